```python
import jax
import jax.numpy as jnp
from jax import lax
import numpy as np

D_MODEL = 1024
BATCH = 2
SEQ = 8192
DEPTH = 2
DEC_BATCH = 32
DEC_SEQ = 8
PAST_LEN = 8192
PAGE_SIZE = 128

N_BRANCH = 4
BRANCH_WIDTH = D_MODEL // 2
HEAD_DIM = 128
N_HEADS = BRANCH_WIDTH // HEAD_DIM
SB_BLOCK = 128
SB_BIAS_INIT = -5.0
POOL_WINDOWS = (2, 4, 8, 16)
POOL_GROUP = BRANCH_WIDTH // len(POOL_WINDOWS)
POOL_STATE = max(POOL_WINDOWS) - 1
RET_CHUNK = 128
DELTA_CHUNK = 64
CONV_WIDTH = 4
ROPE_BASE = 10000.0
EPS = 1e-6
IN_SIZES = (3 * BRANCH_WIDTH, BRANCH_WIDTH, BRANCH_WIDTH, BRANCH_WIDTH, 3 * BRANCH_WIDTH, BRANCH_WIDTH, 3 * BRANCH_WIDTH, N_HEADS, N_HEADS, BRANCH_WIDTH, N_BRANCH * D_MODEL)
IN_TOTAL = sum(IN_SIZES)

kernel_name = 'hybrid_sb_pool_ret_delta_step'


def rms_norm(x, g):
    xf = x.astype(jnp.float32)
    y = xf * lax.rsqrt(jnp.mean(xf * xf, axis=-1, keepdims=True) + EPS)
    return (y * g.astype(jnp.float32)).astype(x.dtype)


def l2_norm(x):
    xf = x.astype(jnp.float32)
    return xf * lax.rsqrt(jnp.sum(xf * xf, axis=-1, keepdims=True) + EPS)


def split_in(u):
    parts, off = [], 0
    for size in IN_SIZES:
        parts.append(u[..., off:off + size])
        off += size
    return parts


def rotary(x, pos):
    half = HEAD_DIM // 2
    inv_freq = ROPE_BASE ** (-jnp.arange(half, dtype=jnp.float32) / half)
    ang = pos.astype(jnp.float32)[:, None] * inv_freq[None, :]
    cos = jnp.cos(ang)[None, :, None, :]
    sin = jnp.sin(ang)[None, :, None, :]
    xf = x.astype(jnp.float32)
    x1, x2 = xf[..., :half], xf[..., half:]
    return jnp.concatenate([x1 * cos - x2 * sin, x1 * sin + x2 * cos], axis=-1)


def stick_breaking(q, k, v, q_pos, k_pos, bias):
    z = jnp.einsum('bqhd,bkhd->bhqk', q, k).astype(jnp.float32) * (HEAD_DIM ** -0.5)
    z = z + bias.astype(jnp.float32)[None, :, None, None]
    vis = (k_pos[None, :] < q_pos[:, None])[None, None]
    log_skip = jnp.where(vis, jax.nn.log_sigmoid(-z), 0.0)
    between = lax.cumsum(log_skip, axis=3, reverse=True) - log_skip
    w = jnp.where(vis, jnp.exp(jax.nn.log_sigmoid(z) + between), 0.0)
    return jnp.einsum('bhqk,bkhd->bqhd', w.astype(v.dtype), v)


def stick_breaking_prompt(q, k, v, bias):
    B, T, H, D = q.shape
    nb = T // SB_BLOCK
    q_blocks = q.reshape(B, nb, SB_BLOCK, H, D).swapaxes(0, 1)
    k_pos = jnp.arange(T)

    def block(args):
        q_blk, bi = args
        return stick_breaking(q_blk, k, v, bi * SB_BLOCK + jnp.arange(SB_BLOCK), k_pos, bias)

    o = lax.map(block, (q_blocks, jnp.arange(nb)))
    return o.swapaxes(0, 1).reshape(B, T, H, D)


def make_sample_attend(k_past, v_past, past):
    def attend(q, k, v, bias):
        T = q.shape[1]
        kk = jnp.concatenate([k_past, k], axis=1)
        vv = jnp.concatenate([v_past, v], axis=1)
        return stick_breaking(q, kk, vv, past + jnp.arange(T), jnp.arange(past + T), bias)
    return attend


def pool_mixer(xb, hist, pos0, w_grp, scale):
    B, T, W = xb.shape
    P = hist.shape[1]
    ext = jnp.concatenate([hist, xb], axis=1)
    ext32 = ext.astype(jnp.float32)
    csum = jnp.concatenate([jnp.zeros((B, 1, W), jnp.float32), jnp.cumsum(ext32, axis=1)], axis=1)
    hi = P + 1 + jnp.arange(T)
    pos = pos0 + jnp.arange(T)
    means = []
    for gi, win in enumerate(POOL_WINDOWS):
        cg = csum[..., gi * POOL_GROUP:(gi + 1) * POOL_GROUP]
        lo = jnp.maximum(hi - win, 0)
        count = jnp.minimum(pos + 1, win).astype(jnp.float32)
        means.append((cg[:, hi] - cg[:, lo]) / count[None, :, None])
    pooled = jnp.concatenate(means, axis=-1) - ext32[:, P:]
    pooled = pooled.reshape(B, T, len(POOL_WINDOWS), POOL_GROUP)
    mixed = jnp.einsum('btgc,gcd->btgd', pooled, w_grp.astype(jnp.float32)).reshape(B, T, W)
    return (mixed * scale.astype(jnp.float32)).astype(xb.dtype), ext[:, -POOL_STATE:]


def retention(q, k, v, s0, chunk):
    B, T, H, D = q.shape
    nc = T // chunk

    def blocks(t):
        return t.astype(jnp.float32).reshape(B, nc, chunk, H, D).transpose(1, 0, 3, 2, 4)

    qb = blocks(q) * (D ** -0.5)
    kb, vb = blocks(k), blocks(v)
    log_gamma = jnp.log(1.0 - 2.0 ** (-5.0 - jnp.arange(H, dtype=jnp.float32)))
    i = jnp.arange(chunk, dtype=jnp.float32)
    diff = i[:, None] - i[None, :]
    dec_in = jnp.where(diff >= 0, jnp.exp(log_gamma[:, None, None] * jnp.maximum(diff, 0.0)), 0.0)
    dec_q = jnp.exp(log_gamma[:, None] * (i + 1.0))
    dec_k = jnp.exp(log_gamma[:, None] * (chunk - 1.0 - i))
    dec_c = jnp.exp(log_gamma * chunk)

    def step(S, inp):
        qc, kc, vc = inp
        s = jnp.einsum('bhid,bhjd->bhij', qc, kc) * dec_in
        o = jnp.einsum('bhij,bhje->bhie', s, vc) + jnp.einsum('bhid,bhde->bhie', qc, S) * dec_q[None, :, :, None]
        S = S * dec_c[None, :, None, None] + jnp.einsum('bhjd,bhje->bhde', kc * dec_k[None, :, :, None], vc)
        return S, o

    S, o = lax.scan(step, s0.astype(jnp.float32), (qb, kb, vb))
    return o.transpose(1, 0, 3, 2, 4).reshape(B, T, H, D), S


def gated_delta(q, k, v, g, beta, s0, chunk):
    B, T, H, D = q.shape
    nc = T // chunk

    def blocks(t):
        return t.astype(jnp.float32).reshape(B, nc, chunk, H, -1).transpose(1, 0, 3, 2, 4)

    qb, kb, vb = blocks(q), blocks(k), blocks(v)
    gb = blocks(g[..., None])[..., 0]
    bb = blocks(beta[..., None])[..., 0]
    idx = jnp.arange(chunk)
    incl = idx[:, None] >= idx[None, :]
    strict = idx[:, None] > idx[None, :]
    eye = jnp.eye(chunk, dtype=jnp.float32)

    def step(S, inp):
        qc, kc, vc, gc, bc = inp
        gcum = jnp.cumsum(gc, axis=-1)
        diff = gcum[..., :, None] - gcum[..., None, :]
        decay = jnp.where(incl, jnp.exp(jnp.where(incl, diff, 0.0)), 0.0)
        kbeta = kc * bc[..., None]
        a_mat = jnp.where(strict, jnp.einsum('bhid,bhjd->bhij', kbeta, kc) * decay, 0.0)
        rhs = jnp.concatenate([vc * bc[..., None], kbeta * jnp.exp(gcum)[..., None]], axis=-1)
        sol = lax.linalg.triangular_solve(a_mat + eye, rhs, left_side=True, lower=True, unit_diagonal=True)
        u, w = sol[..., :D], sol[..., D:]
        v_new = u - jnp.einsum('bhid,bhde->bhie', w, S)
        attn = jnp.einsum('bhid,bhjd->bhij', qc, kc) * decay
        o = jnp.einsum('bhid,bhde->bhie', qc * jnp.exp(gcum)[..., None], S) + jnp.einsum('bhij,bhje->bhie', attn, v_new)
        g_last = gcum[..., -1:]
        S = S * jnp.exp(g_last)[..., None] + jnp.einsum('bhid,bhie->bhde', kc * jnp.exp(g_last - gcum)[..., None], v_new)
        return S, o

    S, o = lax.scan(step, s0.astype(jnp.float32), (qb, kb, vb, gb, bb))
    return o.transpose(1, 0, 3, 2, 4).reshape(B, T, H, D), S


def causal_conv(x_ext, w):
    c = x_ext.shape[-1]
    return lax.conv_general_dilated(x_ext, w[:, None, :].astype(x_ext.dtype), window_strides=(1,), padding='VALID',
                                    dimension_numbers=('NWC', 'WIO', 'NWC'), feature_group_count=c)


def mixer_layer(x, pos0, attend, pool_hist, ret_s0, delta_s0, conv_hist, ret_chunk, delta_chunk,
                norm_g, w_in, sb_q_norm, sb_k_norm, sb_bias, pool_w, pool_scale, ret_norm,
                dn_conv_w, dn_a_log, dn_dt_bias, dn_norm, w_branch, w_out):
    B, T, _ = x.shape
    f32 = jnp.float32
    h = rms_norm(x, norm_g)
    u = h @ w_in
    a_qkv, a_gate, b_in, b_gate, c_qkv, c_gate, d_qkv, d_a, d_b, d_gate, merge_logits = split_in(u)

    def heads(t):
        return t.reshape(B, T, N_HEADS, HEAD_DIM)

    pos = pos0 + jnp.arange(T)
    qa, ka, va = jnp.split(a_qkv, 3, axis=-1)
    qa, ka, va = rms_norm(heads(qa), sb_q_norm), rms_norm(heads(ka), sb_k_norm), heads(va)
    o_a = attend(qa, ka, va, sb_bias).reshape(B, T, BRANCH_WIDTH) * jax.nn.silu(a_gate)
    o_b, pool_new = pool_mixer(b_in, pool_hist, pos0, pool_w, pool_scale)
    o_b = o_b * jax.nn.silu(b_gate)
    qc, kc, vc = jnp.split(c_qkv, 3, axis=-1)
    o_c, ret_new = retention(rotary(heads(qc), pos), rotary(heads(kc), pos), heads(vc), ret_s0, ret_chunk)
    o_c = rms_norm(o_c, ret_norm).astype(x.dtype).reshape(B, T, BRANCH_WIDTH) * jax.nn.silu(c_gate)
    conv_ext = jnp.concatenate([conv_hist, d_qkv], axis=1)
    d_conv = jax.nn.silu(causal_conv(conv_ext, dn_conv_w))
    qd, kd, vd = jnp.split(d_conv, 3, axis=-1)
    qd = l2_norm(heads(qd)) * (HEAD_DIM ** -0.5)
    kd = l2_norm(heads(kd))
    beta = jax.nn.sigmoid(d_b.astype(f32))
    log_decay = -jnp.exp(dn_a_log.astype(f32)) * jax.nn.softplus(d_a.astype(f32) + dn_dt_bias.astype(f32))
    o_d, delta_new = gated_delta(qd, kd, heads(vd), log_decay, beta, delta_s0, delta_chunk)
    o_d = rms_norm(o_d, dn_norm).astype(x.dtype).reshape(B, T, BRANCH_WIDTH) * jax.nn.silu(d_gate)
    branches = jnp.stack([o_a, o_b, o_c, o_d], axis=2)
    proj = jnp.einsum('btnw,nwd->btnd', branches, w_branch)
    gates = jax.nn.sigmoid(merge_logits.reshape(B, T, N_BRANCH, D_MODEL))
    y = x + jnp.sum(gates * proj, axis=2) @ w_out
    new_state = (ka, va, pool_new, ret_new.astype(ret_s0.dtype), delta_new.astype(delta_s0.dtype),
                 conv_ext[:, -(CONV_WIDTH - 1):])
    return y, new_state


def stack_states(states, i):
    return jnp.stack([s[i] for s in states], axis=0)


def setup_inputs(seed: int = 0) -> dict:
    key = jax.random.key(seed)
    ks = jax.random.split(key, 24)
    n_pages = PAST_LEN // PAGE_SIZE
    n_used = DEC_BATCH * n_pages
    n_pool = n_used + n_used // 4
    W = BRANCH_WIDTH
    nrm = jax.random.normal

    def gain(k, shape):
        return 1.0 + 0.02 * nrm(k, shape, jnp.float32)

    page_table = jax.random.permutation(ks[0], n_pool)[:n_used].reshape(DEC_BATCH, n_pages).astype(jnp.int32)
    return {
        'x_prompt': nrm(ks[1], (BATCH, SEQ, D_MODEL), jnp.float32),
        'x_sample': nrm(ks[2], (DEC_BATCH, DEC_SEQ, D_MODEL), jnp.float32),
        'cache_k': nrm(ks[3], (DEPTH, n_pool, PAGE_SIZE, N_HEADS, HEAD_DIM), jnp.float32),
        'cache_v': nrm(ks[4], (DEPTH, n_pool, PAGE_SIZE, N_HEADS, HEAD_DIM), jnp.float32),
        'state_pool': nrm(ks[5], (DEPTH, DEC_BATCH, POOL_STATE, W), jnp.float32),
        'state_ret': nrm(ks[6], (DEPTH, DEC_BATCH, N_HEADS, HEAD_DIM, HEAD_DIM), jnp.float32) * HEAD_DIM ** -0.5,
        'state_delta': nrm(ks[7], (DEPTH, DEC_BATCH, N_HEADS, HEAD_DIM, HEAD_DIM), jnp.float32) * HEAD_DIM ** -0.5,
        'state_conv': nrm(ks[8], (DEPTH, DEC_BATCH, CONV_WIDTH - 1, 3 * W), jnp.float32),
        'page_table': page_table,
        'norm_g': gain(ks[9], (DEPTH, D_MODEL)),
        'w_in': nrm(ks[10], (DEPTH, D_MODEL, IN_TOTAL), jnp.float32) * D_MODEL ** -0.5,
        'sb_q_norm': gain(ks[11], (DEPTH, HEAD_DIM)),
        'sb_k_norm': gain(ks[12], (DEPTH, HEAD_DIM)),
        'sb_bias': SB_BIAS_INIT + 0.5 * nrm(ks[22], (DEPTH, N_HEADS), jnp.float32),
        'pool_w': nrm(ks[13], (DEPTH, len(POOL_WINDOWS), POOL_GROUP, POOL_GROUP), jnp.float32) * POOL_GROUP ** -0.5,
        'pool_scale': gain(ks[14], (DEPTH, W)),
        'ret_norm': gain(ks[15], (DEPTH, HEAD_DIM)),
        'dn_conv_w': nrm(ks[16], (DEPTH, CONV_WIDTH, 3 * W), jnp.float32) * CONV_WIDTH ** -0.5,
        'dn_a_log': jnp.log(jax.random.uniform(ks[17], (DEPTH, N_HEADS), jnp.float32, minval=1.0, maxval=16.0)),
        'dn_dt_bias': jnp.log(jnp.expm1(jax.random.uniform(ks[18], (DEPTH, N_HEADS), jnp.float32, minval=0.001, maxval=0.1))),
        'dn_norm': gain(ks[19], (DEPTH, HEAD_DIM)),
        'w_branch': nrm(ks[20], (DEPTH, N_BRANCH, W, D_MODEL), jnp.float32) * W ** -0.5,
        'w_out': nrm(ks[21], (DEPTH, D_MODEL, D_MODEL), jnp.float32) * D_MODEL ** -0.5,
    }


def reference(x_prompt, x_sample, cache_k, cache_v, state_pool, state_ret, state_delta, state_conv, page_table,
              norm_g, w_in, sb_q_norm, sb_k_norm, sb_bias, pool_w, pool_scale, ret_norm,
              dn_conv_w, dn_a_log, dn_dt_bias, dn_norm, w_branch, w_out):
    n_batch = x_prompt.shape[0]
    n_dec, n_new = x_sample.shape[0], x_sample.shape[1]
    past = page_table.shape[1] * PAGE_SIZE
    dt = x_prompt.dtype
    xp, xs = x_prompt, x_sample
    new_p, new_s = [], []
    for l in range(DEPTH):
        weights = (norm_g[l], w_in[l], sb_q_norm[l], sb_k_norm[l], sb_bias[l], pool_w[l], pool_scale[l], ret_norm[l],
                   dn_conv_w[l], dn_a_log[l], dn_dt_bias[l], dn_norm[l], w_branch[l], w_out[l])
        xp, st_p = mixer_layer(xp, 0, stick_breaking_prompt,
                               jnp.zeros((n_batch, 0, BRANCH_WIDTH), dt),
                               jnp.zeros((n_batch, N_HEADS, HEAD_DIM, HEAD_DIM), jnp.float32),
                               jnp.zeros((n_batch, N_HEADS, HEAD_DIM, HEAD_DIM), jnp.float32),
                               jnp.zeros((n_batch, CONV_WIDTH - 1, 3 * BRANCH_WIDTH), dt),
                               RET_CHUNK, DELTA_CHUNK, *weights)
        new_p.append(st_p)
        k_past = cache_k[l][page_table].reshape(n_dec, past, N_HEADS, HEAD_DIM)
        v_past = cache_v[l][page_table].reshape(n_dec, past, N_HEADS, HEAD_DIM)
        xs, st_s = mixer_layer(xs, past, make_sample_attend(k_past, v_past, past),
                               state_pool[l], state_ret[l], state_delta[l], state_conv[l],
                               n_new, n_new, *weights)
        new_s.append(st_s)
    k_prompt, v_prompt = stack_states(new_p, 0), stack_states(new_p, 1)
    pool_prompt, ret_prompt = stack_states(new_p, 2), stack_states(new_p, 3)
    delta_prompt, conv_prompt = stack_states(new_p, 4), stack_states(new_p, 5)
    k_sample, v_sample = stack_states(new_s, 0), stack_states(new_s, 1)
    pool_sample, ret_sample = stack_states(new_s, 2), stack_states(new_s, 3)
    delta_sample, conv_sample = stack_states(new_s, 4), stack_states(new_s, 5)
    return (xp, xs, k_prompt, v_prompt, pool_prompt, ret_prompt, delta_prompt, conv_prompt,
            k_sample, v_sample, pool_sample, ret_sample, delta_sample, conv_sample)
```

```python
import functools

import jax
import jax.numpy as jnp
import numpy as np
from jax import lax
from jax.experimental import pallas as pl
from jax.experimental.pallas import tpu as pltpu

F32 = jnp.float32
BF16 = jnp.bfloat16

D_MODEL = 1024
BRANCH_WIDTH = 512
HEAD_DIM = 128
N_HEADS = 4
N_BRANCH = 4
PAGE_SIZE = 128
POOL_WINDOWS = (2, 4, 8, 16)
POOL_STATE = 15
CONV_WIDTH = 4
ROPE_BASE = 10000.0
EPS = 1e-6
QK_SCALE = HEAD_DIM ** -0.5

LANE = 128
ROW_TILE = 128
COL_BLOCK = 512
VMEM_LIMIT = 56 * 1024 * 1024

CB_AQ, CB_AK, CB_AV, CB_AG, CB_BIN, CB_BG, CB_CQ, CB_CK, CB_CV, CB_CG = range(10)
CB_DQ, CB_DK, CB_DV, CB_DG, CB_MERGE, CB_DAB = 10, 11, 12, 13, 14, 22
N_COL_BLOCKS = 23


def _params(*sem):
    return pltpu.CompilerParams(dimension_semantics=sem, vmem_limit_bytes=VMEM_LIMIT)


def _dot(a, b):
    return jnp.dot(a, b, preferred_element_type=F32)


def _dot_nt(a, b):
    return lax.dot_general(a, b, (((1,), (1,)), ((), ())), preferred_element_type=F32)


def _dot_tn(a, b):
    return lax.dot_general(a, b, (((0,), (0,)), ((), ())), preferred_element_type=F32)


def _softplus(x):
    return jnp.maximum(x, 0.0) + jnp.log1p(jnp.exp(-jnp.abs(x)))


def _silu(x):
    return x * jax.nn.sigmoid(x)


def _head_rms(x, g):
    return x * lax.rsqrt(jnp.mean(x * x, axis=-1, keepdims=True) + EPS) * g


def _split2(x):
    hi = x.astype(BF16)
    return hi, (x - hi.astype(F32)).astype(BF16)


def _dot_split(a, b):
    return _dot(a[0], b[0]) + (_dot(a[0], b[1]) + _dot(a[1], b[0]))


def _split3(x):
    a = x.astype(BF16)
    r = x - a.astype(F32)
    b = r.astype(BF16)
    c = (r - b.astype(F32)).astype(BF16)
    return a, b, c


def _in_proj_kernel(x_ref, g_ref, w_ref, o_ref, h_ref):
    @pl.when(pl.program_id(1) == 0)
    def _():
        x = x_ref[...]
        h_ref[...] = _head_rms(x, g_ref[...]).astype(BF16)

    o_ref[...] = _dot(h_ref[...], w_ref[...])


def _in_proj(x, g, w_packed, tm):
    n, d = x.shape
    nu = w_packed.shape[1]
    return pl.pallas_call(
        _in_proj_kernel,
        grid=(n // tm, nu // COL_BLOCK),
        in_specs=[pl.BlockSpec((tm, d), lambda i, j: (i, 0)),
                  pl.BlockSpec((1, d), lambda i, j: (0, 0)),
                  pl.BlockSpec((d, COL_BLOCK), lambda i, j: (0, j))],
        out_specs=pl.BlockSpec((tm, COL_BLOCK), lambda i, j: (i, j)),
        out_shape=jax.ShapeDtypeStruct((n, nu), F32),
        scratch_shapes=[pltpu.VMEM((tm, d), BF16)],
        compiler_params=_params("parallel", "arbitrary"),
        name="in_proj",
    )(x, g.reshape(1, d), w_packed)


def _head_norm_kernel(x_ref, g_ref, o_ref):
    g = g_ref[...]
    for h in range(N_HEADS):
        sl = slice(h * HEAD_DIM, (h + 1) * HEAD_DIM)
        o_ref[:, sl] = _head_rms(x_ref[:, sl], g)


def _head_norm(u, col_block, g, tm):
    n = u.shape[0]
    return pl.pallas_call(
        _head_norm_kernel,
        grid=(n // tm,),
        in_specs=[pl.BlockSpec((tm, COL_BLOCK), lambda i: (i, col_block)),
                  pl.BlockSpec((1, HEAD_DIM), lambda i: (0, 0))],
        out_specs=pl.BlockSpec((tm, COL_BLOCK), lambda i: (i, 0)),
        out_shape=jax.ShapeDtypeStruct((n, COL_BLOCK), F32),
        compiler_params=_params("parallel"),
        name="head_norm",
    )(u, g.reshape(1, HEAD_DIM))


def _suffix_ones(n):
    r = lax.broadcasted_iota(jnp.int32, (n, n), 0)
    c = lax.broadcasted_iota(jnp.int32, (n, n), 1)
    return jnp.where(r > c, 1.0, 0.0).astype(BF16)


def _sb_tile(qb, kt, vt, bias, ones_after, carry, vis):
    z = _dot_nt(qb, kt) * QK_SCALE + bias
    sp = _softplus(z)
    ls = -sp if vis is None else jnp.where(vis, -sp, 0.0)
    hi = ls.astype(BF16)
    lo = (ls - hi.astype(F32)).astype(BF16)
    between = _dot(hi, ones_after) + _dot(lo, ones_after) + carry
    w = jnp.exp(z - sp + between)
    if vis is not None:
        w = jnp.where(vis, w, 0.0)
    return _dot(w.astype(BF16), vt), jnp.sum(ls, axis=-1, keepdims=True)


def _attn_prompt_kernel(q_ref, k_ref, v_ref, qg_ref, bias_ref, o_ref, acc_ref, car_ref, *, blk):
    i = pl.program_id(2)
    qb = _head_rms(q_ref[...], qg_ref[...]).astype(BF16)
    bias = bias_ref[...]
    ones_after = _suffix_ones(blk)
    r = lax.broadcasted_iota(jnp.int32, (blk, blk), 0)
    c = lax.broadcasted_iota(jnp.int32, (blk, blk), 1)
    acc_ref[...] = jnp.zeros_like(acc_ref)
    car_ref[...] = jnp.zeros_like(car_ref)

    def body(it, _):
        start = pl.multiple_of((i - it) * blk, blk)
        kt = k_ref[pl.ds(start, blk), :].astype(BF16)
        vt = v_ref[pl.ds(start, blk), :].astype(BF16)
        vis = (c - r) < it * blk
        pv, rs = _sb_tile(qb, kt, vt, bias, ones_after, car_ref[...], vis)
        acc_ref[...] += pv
        car_ref[...] += rs
        return 0

    lax.fori_loop(0, i + 1, body, 0)
    o_ref[...] = acc_ref[...]


def _attn_prompt(u, kn, qg, bias_vec, n_batch, seq, blk):
    nq = seq // blk
    hb = COL_BLOCK // HEAD_DIM
    return pl.pallas_call(
        functools.partial(_attn_prompt_kernel, blk=blk),
        grid=(n_batch, N_HEADS, nq),
        in_specs=[pl.BlockSpec((blk, HEAD_DIM), lambda b, h, i: (b * nq + i, CB_AQ * hb + h)),
                  pl.BlockSpec((seq, HEAD_DIM), lambda b, h, i: (b, h)),
                  pl.BlockSpec((seq, HEAD_DIM), lambda b, h, i: (b, CB_AV * hb + h)),
                  pl.BlockSpec((1, HEAD_DIM), lambda b, h, i: (0, 0)),
                  pl.BlockSpec((1, HEAD_DIM), lambda b, h, i: (0, h))],
        out_specs=pl.BlockSpec((blk, HEAD_DIM), lambda b, h, i: (b * nq + i, h)),
        out_shape=jax.ShapeDtypeStruct((n_batch * seq, COL_BLOCK), F32),
        scratch_shapes=[pltpu.VMEM((blk, HEAD_DIM), F32), pltpu.VMEM((blk, 1), F32)],
        compiler_params=_params("parallel", "parallel", "arbitrary"),
        name="attn_prompt",
    )(u, kn, u, qg.reshape(1, HEAD_DIM), bias_vec)


def _attn_sample_kernel(pt_ref, q_ref, kn_ref, vn_ref, qg_ref, bias_ref, *rest, n_new, pages_per_step):
    page_refs = rest[:2 * pages_per_step]
    o_ref, acc_ref, car_ref, knp_ref, vnp_ref = rest[2 * pages_per_step:]
    j = pl.program_id(1)
    ones_after = _suffix_ones(PAGE_SIZE)
    qg = qg_ref[...]

    def tile(h, kt, vt, vis):
        sl = slice(h * HEAD_DIM, (h + 1) * HEAD_DIM)
        qb = _head_rms(q_ref[:, sl], qg).astype(BF16)
        pv, rs = _sb_tile(qb, kt, vt, bias_ref[:, sl], ones_after, car_ref[:, sl], vis)
        acc_ref[:, sl] += pv
        car_ref[:, sl] += rs

    @pl.when(j == 0)
    def _():
        acc_ref[...] = jnp.zeros_like(acc_ref)
        car_ref[...] = jnp.zeros_like(car_ref)
        knp_ref[...] = jnp.zeros_like(knp_ref)
        vnp_ref[...] = jnp.zeros_like(vnp_ref)
        knp_ref[0:n_new, :] = kn_ref[...]
        vnp_ref[0:n_new, :] = vn_ref[...]
        r = lax.broadcasted_iota(jnp.int32, (n_new, PAGE_SIZE), 0)
        c = lax.broadcasted_iota(jnp.int32, (n_new, PAGE_SIZE), 1)
        for h in range(N_HEADS):
            sl = slice(h * HEAD_DIM, (h + 1) * HEAD_DIM)
            tile(h, knp_ref[:, sl].astype(BF16), vnp_ref[:, sl].astype(BF16), c < r)

    for p in range(pages_per_step):
        k_ref, v_ref = page_refs[p], page_refs[pages_per_step + p]
        for h in range(N_HEADS):
            rows = pl.ds(h, PAGE_SIZE, stride=N_HEADS)
            tile(h, k_ref[0, rows, :].astype(BF16), v_ref[0, rows, :].astype(BF16), None)

    o_ref[...] = acc_ref[...]


def _attn_sample(u, kn, cache_k, cache_v, page_table, layer, qg, bias_vec, n_new, pages_per_step):
    n_dec, n_pages = page_table.shape
    depth, n_pool = cache_k.shape[0], cache_k.shape[1]
    ck = cache_k.reshape(depth * n_pool, PAGE_SIZE * N_HEADS, HEAD_DIM)
    cv = cache_v.reshape(depth * n_pool, PAGE_SIZE * N_HEADS, HEAD_DIM)
    n_steps = n_pages // pages_per_step
    base = layer * n_pool

    def page_map(p):
        def index(b, j, pt):
            logical = (n_steps - 1 - j) * pages_per_step + (pages_per_step - 1 - p)
            return (base + pt[b * n_pages + logical], 0, 0)
        return index

    page_specs = [pl.BlockSpec((1, PAGE_SIZE * N_HEADS, HEAD_DIM), page_map(p)) for p in range(pages_per_step)]
    grid_spec = pltpu.PrefetchScalarGridSpec(
        num_scalar_prefetch=1,
        grid=(n_dec, n_steps),
        in_specs=[pl.BlockSpec((n_new, COL_BLOCK), lambda b, j, pt: (b, CB_AQ)),
                  pl.BlockSpec((n_new, COL_BLOCK), lambda b, j, pt: (b, 0)),
                  pl.BlockSpec((n_new, COL_BLOCK), lambda b, j, pt: (b, CB_AV)),
                  pl.BlockSpec((1, HEAD_DIM), lambda b, j, pt: (0, 0)),
                  pl.BlockSpec((1, COL_BLOCK), lambda b, j, pt: (0, 0))] + page_specs + page_specs,
        out_specs=pl.BlockSpec((n_new, COL_BLOCK), lambda b, j, pt: (b, 0)),
        scratch_shapes=[pltpu.VMEM((n_new, COL_BLOCK), F32), pltpu.VMEM((n_new, COL_BLOCK), F32),
                        pltpu.VMEM((PAGE_SIZE, COL_BLOCK), F32), pltpu.VMEM((PAGE_SIZE, COL_BLOCK), F32)],
    )
    return pl.pallas_call(
        functools.partial(_attn_sample_kernel, n_new=n_new, pages_per_step=pages_per_step),
        grid_spec=grid_spec,
        out_shape=jax.ShapeDtypeStruct((n_dec * n_new, COL_BLOCK), F32),
        compiler_params=_params("parallel", "arbitrary"),
        name="attn_sample",
    )(page_table.reshape(-1), u, kn, u, qg.reshape(1, HEAD_DIM), bias_vec,
      *([ck] * pages_per_step), *([cv] * pages_per_step))


POOL_HALO = 16


def _pool_kernel(x_ref, halo_ref, w_ref, scale_ref, o_ref, buf_ref, *, tm, seq_rows, prefix_rows, pos0):
    i = pl.program_id(0)
    buf_ref[0:POOL_HALO, :] = halo_ref[...]
    buf_ref[POOL_HALO:, :] = x_ref[...]
    row = lax.broadcasted_iota(jnp.int32, (tm, LANE), 0) + i * tm
    pos = row & (seq_rows - 1)
    abs_pos = pos + (pos0 - prefix_rows)
    for g, win in enumerate(POOL_WINDOWS):
        sl = slice(g * LANE, (g + 1) * LANE)
        x = x_ref[:, sl]
        total = x
        for s in range(1, win):
            shifted = buf_ref[POOL_HALO - s:POOL_HALO - s + tm, sl]
            total = total + jnp.where(pos >= s, shifted, 0.0)
        count = jnp.minimum(abs_pos + 1, win).astype(F32)
        pooled = total / count - x
        mixed = _dot(pooled.astype(BF16), w_ref[g].astype(BF16))
        o_ref[:, sl] = mixed * scale_ref[:, sl]


def _pool(x_arr, col_block, pool_w, pool_scale, tm, seq_rows, prefix_rows, pos0):
    n = x_arr.shape[0]
    hb = tm // POOL_HALO
    return pl.pallas_call(
        functools.partial(_pool_kernel, tm=tm, seq_rows=seq_rows, prefix_rows=prefix_rows, pos0=pos0),
        grid=(n // tm,),
        in_specs=[pl.BlockSpec((tm, COL_BLOCK), lambda i: (i, col_block)),
                  pl.BlockSpec((POOL_HALO, COL_BLOCK), lambda i: (jnp.maximum(i * hb - 1, 0), col_block)),
                  pl.BlockSpec((len(POOL_WINDOWS), LANE, LANE), lambda i: (0, 0, 0)),
                  pl.BlockSpec((1, COL_BLOCK), lambda i: (0, 0))],
        out_specs=pl.BlockSpec((tm, COL_BLOCK), lambda i: (i, 0)),
        out_shape=jax.ShapeDtypeStruct((n, COL_BLOCK), F32),
        scratch_shapes=[pltpu.VMEM((tm + POOL_HALO, COL_BLOCK), F32)],
        compiler_params=_params("parallel"),
        name="pool",
    )(x_arr, x_arr, pool_w, pool_scale.reshape(1, COL_BLOCK))


def _rotary(x, cos, sin_signed):
    return x * cos + pltpu.roll(x, HEAD_DIM // 2, 1) * sin_signed


def _retention_kernel(q_ref, k_ref, v_ref, cos_ref, sin_ref, din_ref, dq_ref, dk_ref, dc_ref, ng_ref, *rest,
                      chunk, tiles_per_seq):
    carried = tiles_per_seq > 0
    if carried:
        o_ref, sout_ref, s_ref = rest
    else:
        s0_ref, o_ref, sout_ref = rest
    i = pl.program_id(0)
    n_chunks = ROW_TILE // chunk
    cos, sin = cos_ref[...], sin_ref[...]
    ng = ng_ref[...]
    if carried:
        @pl.when(i % tiles_per_seq == 0)
        def _():
            s_ref[...] = jnp.zeros_like(s_ref)
    row = lax.broadcasted_iota(jnp.int32, (ROW_TILE, LANE), 0)

    for h in range(N_HEADS):
        sl = slice(h * HEAD_DIM, (h + 1) * HEAD_DIM)
        qr = (_rotary(q_ref[:, sl], cos, sin) * QK_SCALE).astype(BF16)
        kr = _rotary(k_ref[:, sl], cos, sin)
        vb = v_ref[:, sl].astype(BF16)
        s = _dot_nt(qr, kr.astype(BF16)) * din_ref[h]
        o = _dot(s.astype(BF16), vb)
        kd = kr * dk_ref[h]
        dq = dq_ref[h]
        dc = dc_ref[h]
        if carried:
            state = s_ref[h]
            o = o + _dot(qr, state.astype(BF16)) * dq
            new_state = state * dc + _dot_tn(kd.astype(BF16), vb)
            s_ref[h] = new_state

            @pl.when(i % tiles_per_seq == tiles_per_seq - 1)
            def _():
                sout_ref[0, h] = new_state
        else:
            inter = []
            for c in range(n_chunks):
                rs = slice(c * chunk, (c + 1) * chunk)
                state = s0_ref[c, h]
                inter.append(_dot(qr[rs], state.astype(BF16)))
                in_chunk = (row >= c * chunk) & (row < (c + 1) * chunk)
                kd_c = jnp.where(in_chunk, kd, 0.0).astype(BF16)
                sout_ref[c, h] = state * dc + _dot_tn(kd_c, vb)
            o = o + jnp.concatenate(inter, axis=0) * dq
        o_ref[:, sl] = _head_rms(o, ng)


def _retention_tables(chunk, pos_rows):
    heads = jnp.arange(N_HEADS, dtype=F32)
    log_gamma = jnp.log(1.0 - 2.0 ** (-5.0 - heads))
    r = jnp.arange(ROW_TILE)
    i = (r % chunk).astype(F32)
    diff = i[:, None] - i[None, :]
    same = (r[:, None] // chunk) == (r[None, :] // chunk)
    dec_in = jnp.where(same & (diff >= 0), jnp.exp(log_gamma[:, None, None] * jnp.maximum(diff, 0.0)), 0.0)
    dec_q = jnp.exp(log_gamma[:, None] * (i + 1.0))
    dec_k = jnp.exp(log_gamma[:, None] * (chunk - 1.0 - i))
    dec_c = jnp.exp(log_gamma * chunk)
    lanes = (N_HEADS, ROW_TILE, LANE)
    half = HEAD_DIM // 2
    inv_freq = ROPE_BASE ** (-jnp.arange(half, dtype=F32) / half)
    ang = pos_rows.astype(F32)[:, None] * inv_freq[None, :]
    cos = jnp.concatenate([jnp.cos(ang), jnp.cos(ang)], axis=-1)
    sin = jnp.concatenate([-jnp.sin(ang), jnp.sin(ang)], axis=-1)
    return (cos, sin, dec_in, jnp.broadcast_to(dec_q[:, :, None], lanes), jnp.broadcast_to(dec_k[:, :, None], lanes),
            jnp.broadcast_to(dec_c[:, None, None], (N_HEADS, 1, LANE)))


def _retention(u, norm_g, chunk, tiles_per_seq, pos_rows, s0=None):
    n = u.shape[0]
    n_tiles = n // ROW_TILE
    carried = tiles_per_seq > 0
    cos, sin, dec_in, dec_q, dec_k, dec_c = _retention_tables(chunk, pos_rows)
    rope_tiles = cos.shape[0] // ROW_TILE
    blk = lambda cb: pl.BlockSpec((ROW_TILE, COL_BLOCK), lambda i: (i, cb))
    rope = pl.BlockSpec((ROW_TILE, HEAD_DIM), lambda i: (i % rope_tiles, 0))
    table = pl.BlockSpec((N_HEADS, ROW_TILE, LANE), lambda i: (0, 0, 0))
    in_specs = [blk(CB_CQ), blk(CB_CK), blk(CB_CV), rope, rope, table, table, table,
                pl.BlockSpec((N_HEADS, 1, LANE), lambda i: (0, 0, 0)),
                pl.BlockSpec((1, HEAD_DIM), lambda i: (0, 0))]
    args = [u, u, u, cos, sin, dec_in, dec_q, dec_k, dec_c, norm_g.reshape(1, HEAD_DIM)]
    if carried:
        n_seq = n_tiles // tiles_per_seq
        state_spec = pl.BlockSpec((1, N_HEADS, HEAD_DIM, HEAD_DIM), lambda i: (i // tiles_per_seq, 0, 0, 0))
        scratch = [pltpu.VMEM((N_HEADS, HEAD_DIM, HEAD_DIM), F32)]
    else:
        n_chunks = ROW_TILE // chunk
        n_seq = n_tiles * n_chunks
        state_spec = pl.BlockSpec((n_chunks, N_HEADS, HEAD_DIM, HEAD_DIM), lambda i: (i, 0, 0, 0))
        in_specs.append(state_spec)
        args.append(s0)
        scratch = []
    return pl.pallas_call(
        functools.partial(_retention_kernel, chunk=chunk, tiles_per_seq=tiles_per_seq),
        grid=(n_tiles,),
        in_specs=in_specs,
        out_specs=[pl.BlockSpec((ROW_TILE, COL_BLOCK), lambda i: (i, 0)), state_spec],
        out_shape=[jax.ShapeDtypeStruct((n, COL_BLOCK), F32),
                   jax.ShapeDtypeStruct((n_seq, N_HEADS, HEAD_DIM, HEAD_DIM), F32)],
        scratch_shapes=scratch,
        compiler_params=_params("arbitrary"),
        name="retention",
    )(*args)


CONV_HALO = 8


def _delta_kernel(xq_ref, xk_ref, xv_ref, hq_ref, hk_ref, hv_ref, ab_ref, cw_ref, alog_ref, dt_ref, ng_ref, *rest,
                  chunk, tiles_per_seq, seq_rows, prefix_rows):
    carried = tiles_per_seq > 0
    if carried:
        o_ref, sout_ref, s_ref, buf_ref = rest
    else:
        s0_ref, o_ref, sout_ref, buf_ref = rest
    i = pl.program_id(0)
    n_chunks = ROW_TILE // chunk
    ng = ng_ref[...]
    if carried:
        @pl.when(i % tiles_per_seq == 0)
        def _():
            s_ref[...] = jnp.zeros_like(s_ref)

    row = lax.broadcasted_iota(jnp.int32, (ROW_TILE, LANE), 0)
    col = lax.broadcasted_iota(jnp.int32, (ROW_TILE, LANE), 1)
    pos = (row + i * ROW_TILE) & (seq_rows - 1)
    real = pos >= prefix_rows
    shift = int(np.log2(chunk))
    same = (row >> shift) == (col >> shift)
    incl = same & (row >= col)
    strict = same & (row > col)
    incl_ones = jnp.where(incl, 1.0, 0.0).astype(BF16)
    same_ones = jnp.where(same, 1.0, 0.0).astype(BF16)

    ab = ab_ref[...]
    g = jnp.where(real, -jnp.exp(alog_ref[...]) * _softplus(ab + dt_ref[...]), 0.0)
    beta_all = jnp.where(real, jax.nn.sigmoid(ab), 0.0)
    g3 = _split3(g)
    gcum = sum(_dot(incl_ones, p) for p in g3)
    glast = sum(_dot(same_ones, p) for p in g3)
    gcum_t = gcum.T
    egc = jnp.exp(gcum)
    erest = jnp.exp(glast - gcum)
    elast = jnp.exp(glast)

    def conv(x_ref, halo_ref, lo):
        buf_ref[0:CONV_HALO, :] = halo_ref[...]
        buf_ref[CONV_HALO:, :] = x_ref[...]
        outs = []
        for h in range(N_HEADS):
            sl = slice(h * HEAD_DIM, (h + 1) * HEAD_DIM)
            wsl = slice(lo + h * HEAD_DIM, lo + (h + 1) * HEAD_DIM)
            out = x_ref[:, sl] * cw_ref[CONV_WIDTH - 1:CONV_WIDTH, wsl]
            for s in range(1, CONV_WIDTH):
                shifted = buf_ref[CONV_HALO - s:CONV_HALO - s + ROW_TILE, sl]
                out = out + jnp.where(pos >= s, shifted, 0.0) * cw_ref[CONV_WIDTH - 1 - s:CONV_WIDTH - s, wsl]
            outs.append(_silu(out))
        return outs

    cq = conv(xq_ref, hq_ref, 0)
    ck = conv(xk_ref, hk_ref, COL_BLOCK)
    cv = conv(xv_ref, hv_ref, 2 * COL_BLOCK)

    for h in range(N_HEADS):
        sl = slice(h * HEAD_DIM, (h + 1) * HEAD_DIM)
        qd, kd, vd = cq[h], ck[h], cv[h]
        qn = qd * lax.rsqrt(jnp.sum(qd * qd, axis=-1, keepdims=True) + EPS) * QK_SCALE
        kn = jnp.where(real, kd * lax.rsqrt(jnp.sum(kd * kd, axis=-1, keepdims=True) + EPS), 0.0)
        beta = beta_all[:, N_HEADS + h:N_HEADS + h + 1]
        gi = gcum[:, h:h + 1]
        gj = gcum_t[h:h + 1, :]
        decay = jnp.where(incl, jnp.exp(jnp.where(incl, gi - gj, 0.0)), 0.0)
        kbeta = kn * beta
        knb = kn.astype(BF16)
        a_mat = jnp.where(strict, _dot_nt(kbeta.astype(BF16), knb) * decay, 0.0)
        y = -a_mat
        p2 = _split2(a_mat)
        for _ in range(int(np.log2(chunk)) - 1):
            p = _dot_split(p2, p2)
            p2 = _split2(p)
            y = y + p + _dot_split(_split2(y), p2)
        rhs = jnp.concatenate([vd * beta, kbeta * egc[:, h:h + 1]], axis=1)
        sol = rhs + _dot(y.astype(BF16), rhs.astype(BF16))
        u_mat, w_mat = sol[:, :HEAD_DIM], sol[:, HEAD_DIM:]
        attn = (_dot_nt(qn.astype(BF16), knb) * decay).astype(BF16)
        qg = (qn * egc[:, h:h + 1]).astype(BF16)
        kdec = kn * erest[:, h:h + 1]
        wb = w_mat.astype(BF16)
        if carried:
            state = s_ref[h]
            sb = state.astype(BF16)
            v_new = u_mat - _dot(wb, sb)
            vnb = v_new.astype(BF16)
            o = _dot(qg, sb) + _dot(attn, vnb)
            new_state = state * elast[0:1, h:h + 1] + _dot_tn(kdec.astype(BF16), vnb)
            s_ref[h] = new_state

            @pl.when(i % tiles_per_seq == tiles_per_seq - 1)
            def _():
                sout_ref[0, h] = new_state
        else:
            ws, qs = [], []
            for c in range(n_chunks):
                rs = slice(c * chunk, (c + 1) * chunk)
                sb = s0_ref[c, h].astype(BF16)
                ws.append(_dot(wb[rs], sb))
                qs.append(_dot(qg[rs], sb))
            v_new = u_mat - jnp.concatenate(ws, axis=0)
            vnb = v_new.astype(BF16)
            o = jnp.concatenate(qs, axis=0) + _dot(attn, vnb)
            for c in range(n_chunks):
                in_chunk = (row >= c * chunk) & (row < (c + 1) * chunk)
                kdec_c = jnp.where(in_chunk, kdec, 0.0).astype(BF16)
                sout_ref[c, h] = s0_ref[c, h] * elast[c * chunk:c * chunk + 1, h:h + 1] + _dot_tn(kdec_c, vnb)
        o_ref[:, sl] = _head_rms(o, ng)


def _delta(x_arr, cb_q, ab_arr, ab_block, conv_w, a_log, dt_bias, norm_g, chunk, tiles_per_seq, seq_rows,
           prefix_rows, s0=None):
    n = x_arr.shape[0]
    n_tiles = n // ROW_TILE
    carried = tiles_per_seq > 0
    hb = ROW_TILE // CONV_HALO
    blk = lambda cb: pl.BlockSpec((ROW_TILE, COL_BLOCK), lambda i: (i, cb))
    halo = lambda cb: pl.BlockSpec((CONV_HALO, COL_BLOCK), lambda i: (jnp.maximum(i * hb - 1, 0), cb))
    vec = pl.BlockSpec((1, LANE), lambda i: (0, 0))
    pad = lambda v: jnp.zeros((1, LANE), F32).at[0, :N_HEADS].set(v)
    in_specs = [blk(cb_q), blk(cb_q + 1), blk(cb_q + 2), halo(cb_q), halo(cb_q + 1), halo(cb_q + 2),
                pl.BlockSpec((ROW_TILE, LANE), lambda i: (i, ab_block)),
                pl.BlockSpec((CONV_WIDTH, 3 * COL_BLOCK), lambda i: (0, 0)), vec, vec,
                pl.BlockSpec((1, HEAD_DIM), lambda i: (0, 0))]
    args = [x_arr] * 6 + [ab_arr, conv_w, pad(a_log), pad(dt_bias), norm_g.reshape(1, HEAD_DIM)]
    scratch = [pltpu.VMEM((ROW_TILE + CONV_HALO, COL_BLOCK), F32)]
    if carried:
        n_seq = n_tiles // tiles_per_seq
        state_spec = pl.BlockSpec((1, N_HEADS, HEAD_DIM, HEAD_DIM), lambda i: (i // tiles_per_seq, 0, 0, 0))
        scratch = [pltpu.VMEM((N_HEADS, HEAD_DIM, HEAD_DIM), F32)] + scratch
    else:
        n_chunks = ROW_TILE // chunk
        n_seq = n_tiles * n_chunks
        state_spec = pl.BlockSpec((n_chunks, N_HEADS, HEAD_DIM, HEAD_DIM), lambda i: (i, 0, 0, 0))
        in_specs.append(state_spec)
        args.append(s0)
    return pl.pallas_call(
        functools.partial(_delta_kernel, chunk=chunk, tiles_per_seq=tiles_per_seq, seq_rows=seq_rows,
                          prefix_rows=prefix_rows),
        grid=(n_tiles,),
        in_specs=in_specs,
        out_specs=[pl.BlockSpec((ROW_TILE, COL_BLOCK), lambda i: (i, 0)), state_spec],
        out_shape=[jax.ShapeDtypeStruct((n, COL_BLOCK), F32),
                   jax.ShapeDtypeStruct((n_seq, N_HEADS, HEAD_DIM, HEAD_DIM), F32)],
        scratch_shapes=scratch,
        compiler_params=_params("arbitrary"),
        name="delta",
    )(*args)


def _out_proj_kernel(x_ref, oa_ref, ob_ref, oc_ref, od_ref, ga_ref, gb_ref, gc_ref, gd_ref,
                     ma_ref, mb_ref, mc_ref, md_ref, wb_ref, wo_ref, y_ref):
    mixed = None
    branches = ((oa_ref, ga_ref, ma_ref), (ob_ref, gb_ref, mb_ref), (oc_ref, gc_ref, mc_ref), (od_ref, gd_ref, md_ref))
    for n, (o_ref, g_ref, m_ref) in enumerate(branches):
        branch = (o_ref[...] * _silu(g_ref[...])).astype(BF16)
        proj = _dot(branch, wb_ref[n])
        gated = jax.nn.sigmoid(m_ref[...]) * proj
        mixed = gated if mixed is None else mixed + gated
    y_ref[...] = x_ref[...] + _dot(mixed.astype(BF16), wo_ref[...])


def _out_proj(x, u, oa, ob, oc, od, w_branch, w_out, tm):
    n = x.shape[0]
    row = lambda width, cb: pl.BlockSpec((tm, width), lambda i: (i, cb))
    merge0 = CB_MERGE * COL_BLOCK // D_MODEL
    return pl.pallas_call(
        _out_proj_kernel,
        grid=(n // tm,),
        in_specs=[row(D_MODEL, 0)] + [row(COL_BLOCK, 0)] * 4
                 + [row(COL_BLOCK, CB_AG), row(COL_BLOCK, CB_BG), row(COL_BLOCK, CB_CG), row(COL_BLOCK, CB_DG)]
                 + [row(D_MODEL, merge0 + b) for b in range(N_BRANCH)]
                 + [pl.BlockSpec((N_BRANCH, BRANCH_WIDTH, D_MODEL), lambda i: (0, 0, 0)),
                    pl.BlockSpec((D_MODEL, D_MODEL), lambda i: (0, 0))],
        out_specs=row(D_MODEL, 0),
        out_shape=jax.ShapeDtypeStruct((n, D_MODEL), F32),
        compiler_params=_params("parallel"),
        name="out_proj",
    )(x, oa, ob, oc, od, *([u] * 8), w_branch, w_out)


def _pack_w_in(w):
    d_ab = 13 * COL_BLOCK
    tail = jnp.zeros((w.shape[0], COL_BLOCK - 2 * N_HEADS), w.dtype)
    return jnp.concatenate([w[:, :d_ab], w[:, d_ab + 2 * N_HEADS:], w[:, d_ab:d_ab + 2 * N_HEADS], tail],
                           axis=1).astype(BF16)


def _pad_front(hist, x, seq_rows):
    n, t, w = x.shape
    p = hist.shape[1]
    zeros = jnp.zeros((n, seq_rows - t - p, w), x.dtype)
    return jnp.concatenate([zeros, hist, x], axis=1).reshape(n * seq_rows, w)


def kernel(x_prompt, x_sample, cache_k, cache_v, state_pool, state_ret, state_delta, state_conv, page_table,
           norm_g, w_in, sb_q_norm, sb_k_norm, sb_bias, pool_w, pool_scale, ret_norm,
           dn_conv_w, dn_a_log, dn_dt_bias, dn_norm, w_branch, w_out):
    n_batch, seq, _ = x_prompt.shape
    n_dec, n_new, _ = x_sample.shape
    depth = norm_g.shape[0]
    past = page_table.shape[1] * PAGE_SIZE
    heads = (N_HEADS, HEAD_DIM)
    pool_rows, delta_rows = 32, 16
    assert (n_dec * n_new) % (2 * ROW_TILE) == 0 and seq % 1024 == 0 and past >= max(POOL_WINDOWS)

    xp = x_prompt.reshape(n_batch * seq, D_MODEL)
    xs = x_sample.reshape(n_dec * n_new, D_MODEL)
    pos_p = jnp.arange(seq)
    pos_s = past + (jnp.arange(ROW_TILE) % n_new)
    new_p, new_s = [], []
    for l in range(depth):
        w_packed = _pack_w_in(w_in[l])
        wb = w_branch[l].astype(BF16)
        wo = w_out[l].astype(BF16)
        bias_vec = jnp.repeat(sb_bias[l], HEAD_DIM).reshape(1, COL_BLOCK)
        ab_block = CB_DAB * COL_BLOCK // LANE

        u = _in_proj(xp, norm_g[l], w_packed, 1024)
        kn = _head_norm(u, CB_AK, sb_k_norm[l], 1024)
        oa = _attn_prompt(u, kn, sb_q_norm[l], bias_vec, n_batch, seq, 128)
        ob = _pool(u, CB_BIN, pool_w[l], pool_scale[l], 512, seq, 0, 0)
        oc, ret = _retention(u, ret_norm[l], ROW_TILE, seq // ROW_TILE, pos_p)
        od, delta = _delta(u, CB_DQ, u, ab_block, dn_conv_w[l], dn_a_log[l], dn_dt_bias[l], dn_norm[l],
                           ROW_TILE, seq // ROW_TILE, seq, 0)
        xp = _out_proj(xp, u, oa, ob, oc, od, wb, wo, 256)
        cols = lambda cb, width=COL_BLOCK: u[:, cb * COL_BLOCK:cb * COL_BLOCK + width].reshape(n_batch, seq, width)
        new_p.append((kn.reshape(n_batch, seq, *heads), cols(CB_AV).reshape(n_batch, seq, *heads),
                      cols(CB_BIN)[:, -POOL_STATE:], ret, delta, cols(CB_DQ, 3 * COL_BLOCK)[:, -(CONV_WIDTH - 1):]))

        u = _in_proj(xs, norm_g[l], w_packed, n_dec * n_new)
        kn = _head_norm(u, CB_AK, sb_k_norm[l], n_dec * n_new)
        oa = _attn_sample(u, kn, cache_k, cache_v, page_table, l, sb_q_norm[l], bias_vec, n_new, 8)
        cols = lambda cb, width=COL_BLOCK: u[:, cb * COL_BLOCK:cb * COL_BLOCK + width].reshape(n_dec, n_new, width)
        b_in, d_qkv = cols(CB_BIN), cols(CB_DQ, 3 * COL_BLOCK)
        ob = _pool(_pad_front(state_pool[l], b_in, pool_rows), 0, pool_w[l], pool_scale[l], 256, pool_rows,
                   pool_rows - n_new, past)
        ob = ob.reshape(n_dec, pool_rows, COL_BLOCK)[:, -n_new:].reshape(n_dec * n_new, COL_BLOCK)
        oc, ret = _retention(u, ret_norm[l], n_new, 0, pos_s, state_ret[l])
        d_ab = u[:, CB_DAB * COL_BLOCK:CB_DAB * COL_BLOCK + LANE].reshape(n_dec, n_new, LANE)
        od, delta = _delta(_pad_front(state_conv[l], d_qkv, delta_rows), 0,
                           _pad_front(jnp.zeros((n_dec, 0, LANE), F32), d_ab, delta_rows), 0,
                           dn_conv_w[l], dn_a_log[l], dn_dt_bias[l], dn_norm[l],
                           delta_rows, 0, delta_rows, delta_rows - n_new, state_delta[l])
        od = od.reshape(n_dec, delta_rows, COL_BLOCK)[:, -n_new:].reshape(n_dec * n_new, COL_BLOCK)
        xs = _out_proj(xs, u, oa, ob, oc, od, wb, wo, 256)
        new_s.append((kn.reshape(n_dec, n_new, *heads), cols(CB_AV).reshape(n_dec, n_new, *heads),
                      jnp.concatenate([state_pool[l], b_in], axis=1)[:, -POOL_STATE:], ret, delta,
                      jnp.concatenate([state_conv[l], d_qkv], axis=1)[:, -(CONV_WIDTH - 1):]))

    stack = lambda states, i: jnp.stack([s[i] for s in states], axis=0)
    return (xp.reshape(n_batch, seq, D_MODEL), xs.reshape(n_dec, n_new, D_MODEL),
            *[stack(new_p, i) for i in range(6)], *[stack(new_s, i) for i in range(6)])
```

```python
import functools

import jax
import jax.numpy as jnp
import numpy as np
from jax import lax
from jax.experimental import pallas as pl
from jax.experimental.pallas import tpu as pltpu

F32 = jnp.float32
BF16 = jnp.bfloat16

D_MODEL = 1024
BRANCH_WIDTH = 512
HEAD_DIM = 128
N_HEADS = 4
N_BRANCH = 4
PAGE_SIZE = 128
POOL_WINDOWS = (2, 4, 8, 16)
POOL_STATE = 15
CONV_WIDTH = 4
ROPE_BASE = 10000.0
EPS = 1e-6
QK_SCALE = HEAD_DIM ** -0.5

LANE = 128
ROW_TILE = 128
COL_BLOCK = 512
VMEM_LIMIT = 56 * 1024 * 1024

CB_AQ, CB_AK, CB_AV, CB_AG, CB_BIN, CB_BG, CB_CQ, CB_CK, CB_CV, CB_CG = range(10)
CB_DQ, CB_DK, CB_DV, CB_DG, CB_MERGE, CB_DAB = 10, 11, 12, 13, 14, 22
N_COL_BLOCKS = 23


def _params(*sem):
    return pltpu.CompilerParams(dimension_semantics=sem, vmem_limit_bytes=VMEM_LIMIT)


def _dot(a, b):
    return jnp.dot(a, b, preferred_element_type=F32)


def _dot_nt(a, b):
    return lax.dot_general(a, b, (((1,), (1,)), ((), ())), preferred_element_type=F32)


def _dot_tn(a, b):
    return lax.dot_general(a, b, (((0,), (0,)), ((), ())), preferred_element_type=F32)


def _softplus(x):
    return jnp.maximum(x, 0.0) + jnp.log1p(jnp.exp(-jnp.abs(x)))


def _silu(x):
    return x * jax.nn.sigmoid(x)


def _head_rms(x, g):
    return x * lax.rsqrt(jnp.mean(x * x, axis=-1, keepdims=True) + EPS) * g


def _split2(x):
    hi = x.astype(BF16)
    return hi, (x - hi.astype(F32)).astype(BF16)


def _dot_split(a, b):
    return _dot(a[0], b[0]) + (_dot(a[0], b[1]) + _dot(a[1], b[0]))


def _split3(x):
    a = x.astype(BF16)
    r = x - a.astype(F32)
    b = r.astype(BF16)
    c = (r - b.astype(F32)).astype(BF16)
    return a, b, c


def _in_proj_kernel(x_ref, g_ref, w_ref, o_ref, h_ref):
    @pl.when(pl.program_id(1) == 0)
    def _():
        x = x_ref[...]
        h_ref[...] = _head_rms(x, g_ref[...]).astype(BF16)

    o_ref[...] = _dot(h_ref[...], w_ref[...])


def _in_proj(x, g, w_packed, tm):
    n, d = x.shape
    nu = w_packed.shape[1]
    return pl.pallas_call(
        _in_proj_kernel,
        grid=(n // tm, nu // COL_BLOCK),
        in_specs=[pl.BlockSpec((tm, d), lambda i, j: (i, 0)),
                  pl.BlockSpec((1, d), lambda i, j: (0, 0)),
                  pl.BlockSpec((d, COL_BLOCK), lambda i, j: (0, j))],
        out_specs=pl.BlockSpec((tm, COL_BLOCK), lambda i, j: (i, j)),
        out_shape=jax.ShapeDtypeStruct((n, nu), F32),
        scratch_shapes=[pltpu.VMEM((tm, d), BF16)],
        compiler_params=_params("parallel", "arbitrary"),
        name="in_proj",
    )(x, g.reshape(1, d), w_packed)


def _head_norm_kernel(x_ref, g_ref, o_ref):
    g = g_ref[...]
    for h in range(N_HEADS):
        sl = slice(h * HEAD_DIM, (h + 1) * HEAD_DIM)
        o_ref[:, sl] = _head_rms(x_ref[:, sl], g)


def _head_norm(u, col_block, g, tm):
    n = u.shape[0]
    return pl.pallas_call(
        _head_norm_kernel,
        grid=(n // tm,),
        in_specs=[pl.BlockSpec((tm, COL_BLOCK), lambda i: (i, col_block)),
                  pl.BlockSpec((1, HEAD_DIM), lambda i: (0, 0))],
        out_specs=pl.BlockSpec((tm, COL_BLOCK), lambda i: (i, 0)),
        out_shape=jax.ShapeDtypeStruct((n, COL_BLOCK), F32),
        compiler_params=_params("parallel"),
        name="head_norm",
    )(u, g.reshape(1, HEAD_DIM))


LOG2E = 1.4426950408889634


def _neg_suffix_ones(n, extra_cols=0):
    r = lax.broadcasted_iota(jnp.int32, (n, n + extra_cols), 0)
    c = lax.broadcasted_iota(jnp.int32, (n, n + extra_cols), 1)
    return jnp.where((r > c) | (c >= n), -1.0, 0.0).astype(BF16)


def _softplus2(z2):
    return jnp.maximum(z2, 0.0) + jnp.log2(1.0 + jnp.exp2(-jnp.abs(z2)))


def _query(q, gain):
    return (_head_rms(q, gain) * (QK_SCALE * LOG2E)).astype(BF16)


def _sb_tile(qb, kt, vt, bias2, neg_after, carry, vis):
    z2 = _dot_nt(qb, kt) + bias2
    sp2 = _softplus2(z2)
    skip = sp2 if vis is None else jnp.where(vis, sp2, 0.0)
    hi, lo = _split2(skip)
    cu = neg_after.shape[0]
    between = []
    for ch in reversed(range(kt.shape[0] // cu)):
        cols = slice(ch * cu, (ch + 1) * cu)
        between.insert(0, _dot(hi[:, cols], neg_after) + _dot(lo[:, cols], neg_after) + carry)
        carry = carry - jnp.sum(skip[:, cols], axis=-1, keepdims=True)
    between = between[0] if len(between) == 1 else jnp.concatenate(between, axis=1)
    w = jnp.exp2(z2 - sp2 + between)
    if vis is not None:
        w = jnp.where(vis, w, 0.0)
    return _dot(w.astype(BF16), vt), carry


def _attn_prompt_kernel(q_ref, k_ref, v_ref, qg_ref, bias_ref, o_ref, qb_ref, acc_ref, car_ref, *, bq, bk, sub):
    i = pl.program_id(2)
    qb_ref[...] = _query(q_ref[...], qg_ref[...])
    bias2 = bias_ref[:, 0:1]
    neg_after = _neg_suffix_ones(sub)
    acc_ref[...] = jnp.zeros_like(acc_ref)
    car_ref[...] = jnp.zeros_like(car_ref)

    def key_block(start, row0, vis):
        kt = k_ref[pl.ds(start, bk), :].astype(BF16)
        vt = v_ref[pl.ds(start, bk), :].astype(BF16)
        rows = slice(row0, bq)
        pv, carry = _sb_tile(qb_ref[rows, :], kt, vt, bias2, neg_after, car_ref[rows, :], vis)
        acc_ref[rows, :] += pv
        car_ref[rows, :] = carry

    for d in reversed(range(bq // bk)):
        r = lax.broadcasted_iota(jnp.int32, (bq - d * bk, bk), 0)
        c = lax.broadcasted_iota(jnp.int32, (bq - d * bk, bk), 1)
        key_block(pl.multiple_of(i * bq + d * bk, bk), d * bk, c < r)

    def body(it, _):
        key_block(pl.multiple_of((i * (bq // bk) - 1 - it) * bk, bk), 0, None)
        return 0

    lax.fori_loop(0, i * (bq // bk), body, 0)
    o_ref[...] = acc_ref[...]


def _attn_prompt(u, kn, qg, bias2_vec, n_batch, seq, bq, bk, sub):
    nq = seq // bq
    hb = COL_BLOCK // HEAD_DIM
    return pl.pallas_call(
        functools.partial(_attn_prompt_kernel, bq=bq, bk=bk, sub=sub),
        grid=(n_batch, N_HEADS, nq),
        in_specs=[pl.BlockSpec((bq, HEAD_DIM), lambda b, h, i: (b * nq + i, CB_AQ * hb + h)),
                  pl.BlockSpec((seq, HEAD_DIM), lambda b, h, i: (b, h)),
                  pl.BlockSpec((seq, HEAD_DIM), lambda b, h, i: (b, CB_AV * hb + h)),
                  pl.BlockSpec((1, HEAD_DIM), lambda b, h, i: (0, 0)),
                  pl.BlockSpec((1, HEAD_DIM), lambda b, h, i: (0, h))],
        out_specs=pl.BlockSpec((bq, HEAD_DIM), lambda b, h, i: (b * nq + i, h)),
        out_shape=jax.ShapeDtypeStruct((n_batch * seq, COL_BLOCK), F32),
        scratch_shapes=[pltpu.VMEM((bq, HEAD_DIM), BF16), pltpu.VMEM((bq, HEAD_DIM), F32),
                        pltpu.VMEM((bq, 1), F32)],
        compiler_params=_params("parallel", "parallel", "arbitrary"),
        name="attn_prompt",
    )(u, kn, u, qg.reshape(1, HEAD_DIM), bias2_vec)


def _attn_sample_kernel(pt_ref, q_ref, kn_ref, vn_ref, qg_ref, bias_ref, *rest, n_new, pages_per_step):
    page_refs = rest[:2 * pages_per_step]
    o_ref, acc_ref, car_ref, knp_ref, vnp_ref = rest[2 * pages_per_step:]
    j = pl.program_id(1)
    qg = qg_ref[...]
    neg_after = _neg_suffix_ones(PAGE_SIZE, LANE)
    head = lambda h: slice(h * HEAD_DIM, (h + 1) * HEAD_DIM)
    qbs = [_query(q_ref[:, head(h)], qg) for h in range(N_HEADS)]

    def tiles(keys, values, vis, carries, accs):
        n_blocks = len(keys)
        z2 = [[_dot_nt(qbs[h], keys[p][h]) + bias_ref[:, head(h)] for h in range(N_HEADS)] for p in range(n_blocks)]
        sp2 = [[_softplus2(z) for z in zs] for zs in z2]
        skip = [[s if vis is None else jnp.where(vis, s, 0.0) for s in ss] for ss in sp2]
        stacked = jnp.concatenate([s for ss in skip for s in ss], axis=0)
        hi, lo = _split2(stacked)
        sums = _dot(hi, neg_after) + _dot(lo, neg_after)
        carries, accs = list(carries), list(accs)
        for p in range(n_blocks):
            for h in range(N_HEADS):
                row0 = (p * N_HEADS + h) * n_new
                blk = sums[row0:row0 + n_new]
                w = jnp.exp2(z2[p][h] - sp2[p][h] + blk[:, :PAGE_SIZE] + carries[h])
                if vis is not None:
                    w = jnp.where(vis, w, 0.0)
                accs[h] = accs[h] + _dot(w.astype(BF16), values[p][h])
                carries[h] = carries[h] + blk[:, PAGE_SIZE:]
        return carries, accs

    @pl.when(j == 0)
    def _():
        knp_ref[...] = jnp.zeros_like(knp_ref)
        vnp_ref[...] = jnp.zeros_like(vnp_ref)
        knp_ref[0:n_new, :] = kn_ref[...]
        vnp_ref[0:n_new, :] = vn_ref[...]
        r = lax.broadcasted_iota(jnp.int32, (n_new, PAGE_SIZE), 0)
        c = lax.broadcasted_iota(jnp.int32, (n_new, PAGE_SIZE), 1)
        zero = jnp.zeros((n_new, HEAD_DIM), F32)
        carries, accs = tiles([[knp_ref[:, head(h)].astype(BF16) for h in range(N_HEADS)]],
                              [[vnp_ref[:, head(h)].astype(BF16) for h in range(N_HEADS)]],
                              c < r, [zero] * N_HEADS, [zero] * N_HEADS)
        for h in range(N_HEADS):
            car_ref[:, head(h)] = carries[h]
            acc_ref[:, head(h)] = accs[h]

    page = lambda ref, h: ref[0, pl.ds(h, PAGE_SIZE, stride=N_HEADS), :].astype(BF16)
    keys = [[page(page_refs[p], h) for h in range(N_HEADS)] for p in range(pages_per_step)]
    values = [[page(page_refs[pages_per_step + p], h) for h in range(N_HEADS)] for p in range(pages_per_step)]
    carries, accs = tiles(keys, values, None, [car_ref[:, head(h)] for h in range(N_HEADS)],
                          [acc_ref[:, head(h)] for h in range(N_HEADS)])
    for h in range(N_HEADS):
        car_ref[:, head(h)] = carries[h]
        acc_ref[:, head(h)] = accs[h]
        o_ref[:, head(h)] = accs[h]


def _attn_sample(u, kn, cache_k, cache_v, page_table, layer, qg, bias2_vec, n_new, pages_per_step):
    n_dec, n_pages = page_table.shape
    depth, n_pool = cache_k.shape[0], cache_k.shape[1]
    ck = cache_k.reshape(depth * n_pool, PAGE_SIZE * N_HEADS, HEAD_DIM)
    cv = cache_v.reshape(depth * n_pool, PAGE_SIZE * N_HEADS, HEAD_DIM)
    n_steps = n_pages // pages_per_step
    base = layer * n_pool

    def page_map(p):
        def index(b, j, pt):
            logical = (n_steps - 1 - j) * pages_per_step + (pages_per_step - 1 - p)
            return (base + pt[b * n_pages + logical], 0, 0)
        return index

    page_specs = [pl.BlockSpec((1, PAGE_SIZE * N_HEADS, HEAD_DIM), page_map(p)) for p in range(pages_per_step)]
    grid_spec = pltpu.PrefetchScalarGridSpec(
        num_scalar_prefetch=1,
        grid=(n_dec, n_steps),
        in_specs=[pl.BlockSpec((n_new, COL_BLOCK), lambda b, j, pt: (b, CB_AQ)),
                  pl.BlockSpec((n_new, COL_BLOCK), lambda b, j, pt: (b, 0)),
                  pl.BlockSpec((n_new, COL_BLOCK), lambda b, j, pt: (b, CB_AV)),
                  pl.BlockSpec((1, HEAD_DIM), lambda b, j, pt: (0, 0)),
                  pl.BlockSpec((1, COL_BLOCK), lambda b, j, pt: (0, 0))] + page_specs + page_specs,
        out_specs=pl.BlockSpec((n_new, COL_BLOCK), lambda b, j, pt: (b, 0)),
        scratch_shapes=[pltpu.VMEM((n_new, COL_BLOCK), F32), pltpu.VMEM((n_new, COL_BLOCK), F32),
                        pltpu.VMEM((PAGE_SIZE, COL_BLOCK), F32), pltpu.VMEM((PAGE_SIZE, COL_BLOCK), F32)],
    )
    return pl.pallas_call(
        functools.partial(_attn_sample_kernel, n_new=n_new, pages_per_step=pages_per_step),
        grid_spec=grid_spec,
        out_shape=jax.ShapeDtypeStruct((n_dec * n_new, COL_BLOCK), F32),
        compiler_params=_params("parallel", "arbitrary"),
        name="attn_sample",
    )(page_table.reshape(-1), u, kn, u, qg.reshape(1, HEAD_DIM), bias2_vec,
      *([ck] * pages_per_step), *([cv] * pages_per_step))


POOL_HALO = 16


def _pool_kernel(x_ref, halo_ref, w_ref, scale_ref, o_ref, buf_ref, *, tm, seq_rows, prefix_rows, pos0):
    i = pl.program_id(0)
    buf_ref[0:POOL_HALO, :] = halo_ref[...]
    buf_ref[POOL_HALO:, :] = x_ref[...]
    row = lax.broadcasted_iota(jnp.int32, (tm, LANE), 0) + i * tm
    pos = row & (seq_rows - 1)
    abs_pos = pos + (pos0 - prefix_rows)
    for g, win in enumerate(POOL_WINDOWS):
        sl = slice(g * LANE, (g + 1) * LANE)
        x = x_ref[:, sl]
        total = x
        for s in range(1, win):
            shifted = buf_ref[POOL_HALO - s:POOL_HALO - s + tm, sl]
            total = total + jnp.where(pos >= s, shifted, 0.0)
        count = jnp.minimum(abs_pos + 1, win).astype(F32)
        pooled = total / count - x
        mixed = _dot(pooled.astype(BF16), w_ref[g].astype(BF16))
        o_ref[:, sl] = mixed * scale_ref[:, sl]


def _pool(x_arr, col_block, pool_w, pool_scale, tm, seq_rows, prefix_rows, pos0):
    n = x_arr.shape[0]
    hb = tm // POOL_HALO
    return pl.pallas_call(
        functools.partial(_pool_kernel, tm=tm, seq_rows=seq_rows, prefix_rows=prefix_rows, pos0=pos0),
        grid=(n // tm,),
        in_specs=[pl.BlockSpec((tm, COL_BLOCK), lambda i: (i, col_block)),
                  pl.BlockSpec((POOL_HALO, COL_BLOCK), lambda i: (jnp.maximum(i * hb - 1, 0), col_block)),
                  pl.BlockSpec((len(POOL_WINDOWS), LANE, LANE), lambda i: (0, 0, 0)),
                  pl.BlockSpec((1, COL_BLOCK), lambda i: (0, 0))],
        out_specs=pl.BlockSpec((tm, COL_BLOCK), lambda i: (i, 0)),
        out_shape=jax.ShapeDtypeStruct((n, COL_BLOCK), F32),
        scratch_shapes=[pltpu.VMEM((tm + POOL_HALO, COL_BLOCK), F32)],
        compiler_params=_params("parallel"),
        name="pool",
    )(x_arr, x_arr, pool_w, pool_scale.reshape(1, COL_BLOCK))


def _rotary(x, cos, sin_signed):
    return x * cos + pltpu.roll(x, HEAD_DIM // 2, 1) * sin_signed


def _retention_kernel(q_ref, k_ref, v_ref, cos_ref, sin_ref, din_ref, dq_ref, dk_ref, dc_ref, ng_ref, *rest,
                      chunk, tiles_per_seq):
    carried = tiles_per_seq > 0
    if carried:
        o_ref, sout_ref, s_ref = rest
    else:
        s0_ref, o_ref, sout_ref = rest
    i = pl.program_id(0)
    n_chunks = ROW_TILE // chunk
    cos, sin = cos_ref[...], sin_ref[...]
    ng = ng_ref[...]
    if carried:
        @pl.when(i % tiles_per_seq == 0)
        def _():
            s_ref[...] = jnp.zeros_like(s_ref)
    row = lax.broadcasted_iota(jnp.int32, (ROW_TILE, LANE), 0)

    heads = range(N_HEADS)
    head = lambda h: slice(h * HEAD_DIM, (h + 1) * HEAD_DIM)
    qr = [(_rotary(q_ref[:, head(h)], cos, sin) * QK_SCALE).astype(BF16) for h in heads]
    kr = [_rotary(k_ref[:, head(h)], cos, sin) for h in heads]
    vb = [v_ref[:, head(h)].astype(BF16) for h in heads]
    s = [(_dot_nt(qr[h], kr[h].astype(BF16)) * din_ref[h]).astype(BF16) for h in heads]
    kd = [kr[h] * dk_ref[h] for h in heads]
    if carried:
        state = [s_ref[h] for h in heads]
        inter = [_dot(qr[h], state[h].astype(BF16)) for h in heads]
        new_state = [state[h] * dc_ref[h] + _dot_tn(kd[h].astype(BF16), vb[h]) for h in heads]
        for h in heads:
            s_ref[h] = new_state[h]

        @pl.when(i % tiles_per_seq == tiles_per_seq - 1)
        def _():
            for h in heads:
                sout_ref[0, h] = new_state[h]
    else:
        chunks = range(n_chunks)
        rows = lambda c: slice(c * chunk, (c + 1) * chunk)
        inter = [jnp.concatenate([_dot(qr[h][rows(c)], s0_ref[c, h].astype(BF16)) for c in chunks], axis=0)
                 for h in heads]
        for h in heads:
            for c in chunks:
                in_chunk = (row >= c * chunk) & (row < (c + 1) * chunk)
                kd_c = jnp.where(in_chunk, kd[h], 0.0).astype(BF16)
                sout_ref[c, h] = s0_ref[c, h] * dc_ref[h] + _dot_tn(kd_c, vb[h])
    for h in heads:
        o = _dot(s[h], vb[h]) + inter[h] * dq_ref[h]
        o_ref[:, head(h)] = _head_rms(o, ng)


def _retention_tables(chunk, pos_rows):
    heads = jnp.arange(N_HEADS, dtype=F32)
    log_gamma = jnp.log(1.0 - 2.0 ** (-5.0 - heads))
    r = jnp.arange(ROW_TILE)
    i = (r % chunk).astype(F32)
    diff = i[:, None] - i[None, :]
    same = (r[:, None] // chunk) == (r[None, :] // chunk)
    dec_in = jnp.where(same & (diff >= 0), jnp.exp(log_gamma[:, None, None] * jnp.maximum(diff, 0.0)), 0.0)
    dec_q = jnp.exp(log_gamma[:, None] * (i + 1.0))
    dec_k = jnp.exp(log_gamma[:, None] * (chunk - 1.0 - i))
    dec_c = jnp.exp(log_gamma * chunk)
    lanes = (N_HEADS, ROW_TILE, LANE)
    half = HEAD_DIM // 2
    inv_freq = ROPE_BASE ** (-jnp.arange(half, dtype=F32) / half)
    ang = pos_rows.astype(F32)[:, None] * inv_freq[None, :]
    cos = jnp.concatenate([jnp.cos(ang), jnp.cos(ang)], axis=-1)
    sin = jnp.concatenate([-jnp.sin(ang), jnp.sin(ang)], axis=-1)
    return (cos, sin, dec_in, jnp.broadcast_to(dec_q[:, :, None], lanes), jnp.broadcast_to(dec_k[:, :, None], lanes),
            jnp.broadcast_to(dec_c[:, None, None], (N_HEADS, 1, LANE)))


def _retention(u, norm_g, chunk, tiles_per_seq, pos_rows, s0=None):
    n = u.shape[0]
    n_tiles = n // ROW_TILE
    carried = tiles_per_seq > 0
    cos, sin, dec_in, dec_q, dec_k, dec_c = _retention_tables(chunk, pos_rows)
    rope_tiles = cos.shape[0] // ROW_TILE
    blk = lambda cb: pl.BlockSpec((ROW_TILE, COL_BLOCK), lambda i: (i, cb))
    rope = pl.BlockSpec((ROW_TILE, HEAD_DIM), lambda i: (i % rope_tiles, 0))
    table = pl.BlockSpec((N_HEADS, ROW_TILE, LANE), lambda i: (0, 0, 0))
    in_specs = [blk(CB_CQ), blk(CB_CK), blk(CB_CV), rope, rope, table, table, table,
                pl.BlockSpec((N_HEADS, 1, LANE), lambda i: (0, 0, 0)),
                pl.BlockSpec((1, HEAD_DIM), lambda i: (0, 0))]
    args = [u, u, u, cos, sin, dec_in, dec_q, dec_k, dec_c, norm_g.reshape(1, HEAD_DIM)]
    if carried:
        n_seq = n_tiles // tiles_per_seq
        state_spec = pl.BlockSpec((1, N_HEADS, HEAD_DIM, HEAD_DIM), lambda i: (i // tiles_per_seq, 0, 0, 0))
        scratch = [pltpu.VMEM((N_HEADS, HEAD_DIM, HEAD_DIM), F32)]
    else:
        n_chunks = ROW_TILE // chunk
        n_seq = n_tiles * n_chunks
        state_spec = pl.BlockSpec((n_chunks, N_HEADS, HEAD_DIM, HEAD_DIM), lambda i: (i, 0, 0, 0))
        in_specs.append(state_spec)
        args.append(s0)
        scratch = []
    return pl.pallas_call(
        functools.partial(_retention_kernel, chunk=chunk, tiles_per_seq=tiles_per_seq),
        grid=(n_tiles,),
        in_specs=in_specs,
        out_specs=[pl.BlockSpec((ROW_TILE, COL_BLOCK), lambda i: (i, 0)), state_spec],
        out_shape=[jax.ShapeDtypeStruct((n, COL_BLOCK), F32),
                   jax.ShapeDtypeStruct((n_seq, N_HEADS, HEAD_DIM, HEAD_DIM), F32)],
        scratch_shapes=scratch,
        compiler_params=_params("arbitrary"),
        name="retention",
    )(*args)


CONV_HALO = 8


def _delta_kernel(xq_ref, xk_ref, xv_ref, hq_ref, hk_ref, hv_ref, ab_ref, cw_ref, alog_ref, dt_ref, ng_ref, *rest,
                  chunk, tiles_per_seq, seq_rows, prefix_rows):
    carried = tiles_per_seq > 0
    if carried:
        o_ref, sout_ref, s_ref, buf_ref = rest
    else:
        s0_ref, o_ref, sout_ref, buf_ref = rest
    i = pl.program_id(0)
    n_chunks = ROW_TILE // chunk
    ng = ng_ref[...]
    if carried:
        @pl.when(i % tiles_per_seq == 0)
        def _():
            s_ref[...] = jnp.zeros_like(s_ref)

    row = lax.broadcasted_iota(jnp.int32, (ROW_TILE, LANE), 0)
    col = lax.broadcasted_iota(jnp.int32, (ROW_TILE, LANE), 1)
    pos = (row + i * ROW_TILE) & (seq_rows - 1)
    real = pos >= prefix_rows
    shift = int(np.log2(chunk))
    same = (row >> shift) == (col >> shift)
    incl = same & (row >= col)
    strict = same & (row > col)
    incl_ones = jnp.where(incl, 1.0, 0.0).astype(BF16)
    same_ones = jnp.where(same, 1.0, 0.0).astype(BF16)

    ab = ab_ref[...]
    g = jnp.where(real, -jnp.exp(alog_ref[...]) * _softplus(ab + dt_ref[...]), 0.0)
    beta_all = jnp.where(real, jax.nn.sigmoid(ab), 0.0)
    g3 = _split3(g)
    gcum = sum(_dot(incl_ones, p) for p in g3)
    glast = sum(_dot(same_ones, p) for p in g3)
    gcum_t = gcum.T
    egc = jnp.exp(gcum)
    erest = jnp.exp(glast - gcum)
    elast = jnp.exp(glast)

    def conv(x_ref, halo_ref, lo):
        buf_ref[0:CONV_HALO, :] = halo_ref[...]
        buf_ref[CONV_HALO:, :] = x_ref[...]
        outs = []
        for h in range(N_HEADS):
            sl = slice(h * HEAD_DIM, (h + 1) * HEAD_DIM)
            wsl = slice(lo + h * HEAD_DIM, lo + (h + 1) * HEAD_DIM)
            out = x_ref[:, sl] * cw_ref[CONV_WIDTH - 1:CONV_WIDTH, wsl]
            for s in range(1, CONV_WIDTH):
                shifted = buf_ref[CONV_HALO - s:CONV_HALO - s + ROW_TILE, sl]
                out = out + jnp.where(pos >= s, shifted, 0.0) * cw_ref[CONV_WIDTH - 1 - s:CONV_WIDTH - s, wsl]
            outs.append(_silu(out))
        return outs

    cq = conv(xq_ref, hq_ref, 0)
    ck = conv(xk_ref, hk_ref, COL_BLOCK)
    cv = conv(xv_ref, hv_ref, 2 * COL_BLOCK)

    heads = range(N_HEADS)
    col = lambda m, h: m[:, h:h + 1]
    l2 = lambda x: x * lax.rsqrt(jnp.sum(x * x, axis=-1, keepdims=True) + EPS)
    qn = [l2(cq[h]) * QK_SCALE for h in heads]
    kn = [jnp.where(real, l2(ck[h]), 0.0) for h in heads]
    beta = [col(beta_all, N_HEADS + h) for h in heads]
    decay = [jnp.where(incl, jnp.exp(jnp.where(incl, col(gcum, h) - gcum_t[h:h + 1, :], 0.0)), 0.0) for h in heads]
    kbeta = [kn[h] * beta[h] for h in heads]
    knb = [kn[h].astype(BF16) for h in heads]
    a_mat = [jnp.where(strict, _dot_nt(kbeta[h].astype(BF16), knb[h]) * decay[h], 0.0) for h in heads]
    y = [-a for a in a_mat]
    p2 = [_split2(a) for a in a_mat]
    for _ in range(int(np.log2(chunk)) - 1):
        p = [_dot_split(p2[h], p2[h]) for h in heads]
        p2 = [_split2(p[h]) for h in heads]
        y = [y[h] + p[h] + _dot_split(_split2(y[h]), p2[h]) for h in heads]
    rhs = [jnp.concatenate([cv[h] * beta[h], kbeta[h] * col(egc, h)], axis=1) for h in heads]
    sol = [rhs[h] + _dot(y[h].astype(BF16), rhs[h].astype(BF16)) for h in heads]
    u_mat = [s[:, :HEAD_DIM] for s in sol]
    wb = [s[:, HEAD_DIM:].astype(BF16) for s in sol]
    attn = [(_dot_nt(qn[h].astype(BF16), knb[h]) * decay[h]).astype(BF16) for h in heads]
    qg = [(qn[h] * col(egc, h)).astype(BF16) for h in heads]
    kdec = [kn[h] * col(erest, h) for h in heads]
    if carried:
        state = [s_ref[h] for h in heads]
        sb = [st.astype(BF16) for st in state]
        vnb = [(u_mat[h] - _dot(wb[h], sb[h])).astype(BF16) for h in heads]
        o = [_dot(qg[h], sb[h]) + _dot(attn[h], vnb[h]) for h in heads]
        new_state = [state[h] * elast[0:1, h:h + 1] + _dot_tn(kdec[h].astype(BF16), vnb[h]) for h in heads]
        for h in heads:
            s_ref[h] = new_state[h]

        @pl.when(i % tiles_per_seq == tiles_per_seq - 1)
        def _():
            for h in heads:
                sout_ref[0, h] = new_state[h]
    else:
        chunks = range(n_chunks)
        rows = lambda c: slice(c * chunk, (c + 1) * chunk)
        sb = [[s0_ref[c, h].astype(BF16) for c in chunks] for h in heads]
        ws = [jnp.concatenate([_dot(wb[h][rows(c)], sb[h][c]) for c in chunks], axis=0) for h in heads]
        vnb = [(u_mat[h] - ws[h]).astype(BF16) for h in heads]
        qs = [jnp.concatenate([_dot(qg[h][rows(c)], sb[h][c]) for c in chunks], axis=0) for h in heads]
        o = [qs[h] + _dot(attn[h], vnb[h]) for h in heads]
        for h in heads:
            for c in chunks:
                in_chunk = (row >= c * chunk) & (row < (c + 1) * chunk)
                kdec_c = jnp.where(in_chunk, kdec[h], 0.0).astype(BF16)
                sout_ref[c, h] = s0_ref[c, h] * elast[c * chunk:c * chunk + 1, h:h + 1] + _dot_tn(kdec_c, vnb[h])
    for h in heads:
        o_ref[:, h * HEAD_DIM:(h + 1) * HEAD_DIM] = _head_rms(o[h], ng)


def _delta(x_arr, cb_q, ab_arr, ab_block, conv_w, a_log, dt_bias, norm_g, chunk, tiles_per_seq, seq_rows,
           prefix_rows, s0=None):
    n = x_arr.shape[0]
    n_tiles = n // ROW_TILE
    carried = tiles_per_seq > 0
    hb = ROW_TILE // CONV_HALO
    blk = lambda cb: pl.BlockSpec((ROW_TILE, COL_BLOCK), lambda i: (i, cb))
    halo = lambda cb: pl.BlockSpec((CONV_HALO, COL_BLOCK), lambda i: (jnp.maximum(i * hb - 1, 0), cb))
    vec = pl.BlockSpec((1, LANE), lambda i: (0, 0))
    pad = lambda v: jnp.zeros((1, LANE), F32).at[0, :N_HEADS].set(v)
    in_specs = [blk(cb_q), blk(cb_q + 1), blk(cb_q + 2), halo(cb_q), halo(cb_q + 1), halo(cb_q + 2),
                pl.BlockSpec((ROW_TILE, LANE), lambda i: (i, ab_block)),
                pl.BlockSpec((CONV_WIDTH, 3 * COL_BLOCK), lambda i: (0, 0)), vec, vec,
                pl.BlockSpec((1, HEAD_DIM), lambda i: (0, 0))]
    args = [x_arr] * 6 + [ab_arr, conv_w, pad(a_log), pad(dt_bias), norm_g.reshape(1, HEAD_DIM)]
    scratch = [pltpu.VMEM((ROW_TILE + CONV_HALO, COL_BLOCK), F32)]
    if carried:
        n_seq = n_tiles // tiles_per_seq
        state_spec = pl.BlockSpec((1, N_HEADS, HEAD_DIM, HEAD_DIM), lambda i: (i // tiles_per_seq, 0, 0, 0))
        scratch = [pltpu.VMEM((N_HEADS, HEAD_DIM, HEAD_DIM), F32)] + scratch
    else:
        n_chunks = ROW_TILE // chunk
        n_seq = n_tiles * n_chunks
        state_spec = pl.BlockSpec((n_chunks, N_HEADS, HEAD_DIM, HEAD_DIM), lambda i: (i, 0, 0, 0))
        in_specs.append(state_spec)
        args.append(s0)
    return pl.pallas_call(
        functools.partial(_delta_kernel, chunk=chunk, tiles_per_seq=tiles_per_seq, seq_rows=seq_rows,
                          prefix_rows=prefix_rows),
        grid=(n_tiles,),
        in_specs=in_specs,
        out_specs=[pl.BlockSpec((ROW_TILE, COL_BLOCK), lambda i: (i, 0)), state_spec],
        out_shape=[jax.ShapeDtypeStruct((n, COL_BLOCK), F32),
                   jax.ShapeDtypeStruct((n_seq, N_HEADS, HEAD_DIM, HEAD_DIM), F32)],
        scratch_shapes=scratch,
        compiler_params=_params("arbitrary"),
        name="delta",
    )(*args)


def _out_proj_kernel(x_ref, oa_ref, ob_ref, oc_ref, od_ref, ga_ref, gb_ref, gc_ref, gd_ref,
                     ma_ref, mb_ref, mc_ref, md_ref, wb_ref, wo_ref, y_ref):
    mixed = None
    branches = ((oa_ref, ga_ref, ma_ref), (ob_ref, gb_ref, mb_ref), (oc_ref, gc_ref, mc_ref), (od_ref, gd_ref, md_ref))
    for n, (o_ref, g_ref, m_ref) in enumerate(branches):
        branch = (o_ref[...] * _silu(g_ref[...])).astype(BF16)
        proj = _dot(branch, wb_ref[n])
        gated = jax.nn.sigmoid(m_ref[...]) * proj
        mixed = gated if mixed is None else mixed + gated
    y_ref[...] = x_ref[...] + _dot(mixed.astype(BF16), wo_ref[...])


def _out_proj(x, u, oa, ob, oc, od, w_branch, w_out, tm):
    n = x.shape[0]
    row = lambda width, cb: pl.BlockSpec((tm, width), lambda i: (i, cb))
    merge0 = CB_MERGE * COL_BLOCK // D_MODEL
    return pl.pallas_call(
        _out_proj_kernel,
        grid=(n // tm,),
        in_specs=[row(D_MODEL, 0)] + [row(COL_BLOCK, 0)] * 4
                 + [row(COL_BLOCK, CB_AG), row(COL_BLOCK, CB_BG), row(COL_BLOCK, CB_CG), row(COL_BLOCK, CB_DG)]
                 + [row(D_MODEL, merge0 + b) for b in range(N_BRANCH)]
                 + [pl.BlockSpec((N_BRANCH, BRANCH_WIDTH, D_MODEL), lambda i: (0, 0, 0)),
                    pl.BlockSpec((D_MODEL, D_MODEL), lambda i: (0, 0))],
        out_specs=row(D_MODEL, 0),
        out_shape=jax.ShapeDtypeStruct((n, D_MODEL), F32),
        compiler_params=_params("parallel"),
        name="out_proj",
    )(x, oa, ob, oc, od, *([u] * 8), w_branch, w_out)


def _pack_w_in(w):
    d_ab = 13 * COL_BLOCK
    tail = jnp.zeros((w.shape[0], COL_BLOCK - 2 * N_HEADS), w.dtype)
    return jnp.concatenate([w[:, :d_ab], w[:, d_ab + 2 * N_HEADS:], w[:, d_ab:d_ab + 2 * N_HEADS], tail],
                           axis=1).astype(BF16)


def _pad_front(hist, x, seq_rows):
    n, t, w = x.shape
    p = hist.shape[1]
    zeros = jnp.zeros((n, seq_rows - t - p, w), x.dtype)
    return jnp.concatenate([zeros, hist, x], axis=1).reshape(n * seq_rows, w)


def kernel(x_prompt, x_sample, cache_k, cache_v, state_pool, state_ret, state_delta, state_conv, page_table,
           norm_g, w_in, sb_q_norm, sb_k_norm, sb_bias, pool_w, pool_scale, ret_norm,
           dn_conv_w, dn_a_log, dn_dt_bias, dn_norm, w_branch, w_out):
    n_batch, seq, _ = x_prompt.shape
    n_dec, n_new, _ = x_sample.shape
    depth = norm_g.shape[0]
    past = page_table.shape[1] * PAGE_SIZE
    heads = (N_HEADS, HEAD_DIM)
    pool_rows, delta_rows = 32, 16
    assert (n_dec * n_new) % (2 * ROW_TILE) == 0 and seq % 1024 == 0 and past >= max(POOL_WINDOWS)

    xp = x_prompt.reshape(n_batch * seq, D_MODEL)
    xs = x_sample.reshape(n_dec * n_new, D_MODEL)
    pos_p = jnp.arange(seq)
    pos_s = past + (jnp.arange(ROW_TILE) % n_new)
    new_p, new_s = [], []
    for l in range(depth):
        w_packed = _pack_w_in(w_in[l])
        wb = w_branch[l].astype(BF16)
        wo = w_out[l].astype(BF16)
        bias2_vec = jnp.repeat(sb_bias[l] * LOG2E, HEAD_DIM).reshape(1, COL_BLOCK)
        ab_block = CB_DAB * COL_BLOCK // LANE

        u = _in_proj(xp, norm_g[l], w_packed, 1024)
        kn = _head_norm(u, CB_AK, sb_k_norm[l], 1024)
        oa = _attn_prompt(u, kn, sb_q_norm[l], bias2_vec, n_batch, seq, 512, 512, 256)
        ob = _pool(u, CB_BIN, pool_w[l], pool_scale[l], 512, seq, 0, 0)
        oc, ret = _retention(u, ret_norm[l], ROW_TILE, seq // ROW_TILE, pos_p)
        od, delta = _delta(u, CB_DQ, u, ab_block, dn_conv_w[l], dn_a_log[l], dn_dt_bias[l], dn_norm[l],
                           ROW_TILE, seq // ROW_TILE, seq, 0)
        xp = _out_proj(xp, u, oa, ob, oc, od, wb, wo, 256)
        cols = lambda cb, width=COL_BLOCK: u[:, cb * COL_BLOCK:cb * COL_BLOCK + width].reshape(n_batch, seq, width)
        new_p.append((kn.reshape(n_batch, seq, *heads), cols(CB_AV).reshape(n_batch, seq, *heads),
                      cols(CB_BIN)[:, -POOL_STATE:], ret, delta, cols(CB_DQ, 3 * COL_BLOCK)[:, -(CONV_WIDTH - 1):]))

        u = _in_proj(xs, norm_g[l], w_packed, n_dec * n_new)
        kn = _head_norm(u, CB_AK, sb_k_norm[l], n_dec * n_new)
        oa = _attn_sample(u, kn, cache_k, cache_v, page_table, l, sb_q_norm[l], bias2_vec, n_new, 8)
        cols = lambda cb, width=COL_BLOCK: u[:, cb * COL_BLOCK:cb * COL_BLOCK + width].reshape(n_dec, n_new, width)
        b_in, d_qkv = cols(CB_BIN), cols(CB_DQ, 3 * COL_BLOCK)
        ob = _pool(_pad_front(state_pool[l], b_in, pool_rows), 0, pool_w[l], pool_scale[l], 256, pool_rows,
                   pool_rows - n_new, past)
        ob = ob.reshape(n_dec, pool_rows, COL_BLOCK)[:, -n_new:].reshape(n_dec * n_new, COL_BLOCK)
        oc, ret = _retention(u, ret_norm[l], n_new, 0, pos_s, state_ret[l])
        d_ab = u[:, CB_DAB * COL_BLOCK:CB_DAB * COL_BLOCK + LANE].reshape(n_dec, n_new, LANE)
        od, delta = _delta(_pad_front(state_conv[l], d_qkv, delta_rows), 0,
                           _pad_front(jnp.zeros((n_dec, 0, LANE), F32), d_ab, delta_rows), 0,
                           dn_conv_w[l], dn_a_log[l], dn_dt_bias[l], dn_norm[l],
                           delta_rows, 0, delta_rows, delta_rows - n_new, state_delta[l])
        od = od.reshape(n_dec, delta_rows, COL_BLOCK)[:, -n_new:].reshape(n_dec * n_new, COL_BLOCK)
        xs = _out_proj(xs, u, oa, ob, oc, od, wb, wo, 256)
        new_s.append((kn.reshape(n_dec, n_new, *heads), cols(CB_AV).reshape(n_dec, n_new, *heads),
                      jnp.concatenate([state_pool[l], b_in], axis=1)[:, -POOL_STATE:], ret, delta,
                      jnp.concatenate([state_conv[l], d_qkv], axis=1)[:, -(CONV_WIDTH - 1):]))

    stack = lambda states, i: jnp.stack([s[i] for s in states], axis=0)
    return (xp.reshape(n_batch, seq, D_MODEL), xs.reshape(n_dec, n_new, D_MODEL),
            *[stack(new_p, i) for i in range(6)], *[stack(new_s, i) for i in range(6)])
```

```python
import functools

import jax
import jax.numpy as jnp
import numpy as np
from jax import lax
from jax.experimental import pallas as pl
from jax.experimental.pallas import tpu as pltpu

F32 = jnp.float32
BF16 = jnp.bfloat16

D_MODEL = 1024
BRANCH_WIDTH = 512
HEAD_DIM = 128
N_HEADS = 4
N_BRANCH = 4
PAGE_SIZE = 128
POOL_WINDOWS = (2, 4, 8, 16)
POOL_STATE = 15
CONV_WIDTH = 4
ROPE_BASE = 10000.0
EPS = 1e-6
QK_SCALE = HEAD_DIM ** -0.5

LANE = 128
ROW_TILE = 128
COL_BLOCK = 512
VMEM_LIMIT = 56 * 1024 * 1024

CB_AQ, CB_AK, CB_AV, CB_AG, CB_BIN, CB_BG, CB_CQ, CB_CK, CB_CV, CB_CG = range(10)
CB_DQ, CB_DK, CB_DV, CB_DG, CB_MERGE, CB_DAB = 10, 11, 12, 13, 14, 22
N_COL_BLOCKS = 23


def _params(*sem):
    return pltpu.CompilerParams(dimension_semantics=sem, vmem_limit_bytes=VMEM_LIMIT)


def _dot(a, b):
    return jnp.dot(a, b, preferred_element_type=F32)


def _dot_nt(a, b):
    return lax.dot_general(a, b, (((1,), (1,)), ((), ())), preferred_element_type=F32)


def _dot_tn(a, b):
    return lax.dot_general(a, b, (((0,), (0,)), ((), ())), preferred_element_type=F32)


def _softplus(x):
    return jnp.maximum(x, 0.0) + jnp.log1p(jnp.exp(-jnp.abs(x)))


def _silu(x):
    return x * jax.nn.sigmoid(x)


def _head_rms(x, g):
    return x * lax.rsqrt(jnp.mean(x * x, axis=-1, keepdims=True) + EPS) * g


def _split2(x):
    hi = x.astype(BF16)
    return hi, (x - hi.astype(F32)).astype(BF16)


def _dot_split(a, b):
    return _dot(a[0], b[0]) + (_dot(a[0], b[1]) + _dot(a[1], b[0]))


def _split3(x):
    a = x.astype(BF16)
    r = x - a.astype(F32)
    b = r.astype(BF16)
    c = (r - b.astype(F32)).astype(BF16)
    return a, b, c


def _in_proj_kernel(x_ref, g_ref, w_ref, o_ref, h_ref):
    @pl.when(pl.program_id(1) == 0)
    def _():
        x = x_ref[...]
        h_ref[...] = _head_rms(x, g_ref[...]).astype(BF16)

    o_ref[...] = _dot(h_ref[...], w_ref[...])


def _in_proj(x, g, w_packed, tm):
    n, d = x.shape
    nu = w_packed.shape[1]
    return pl.pallas_call(
        _in_proj_kernel,
        grid=(n // tm, nu // COL_BLOCK),
        in_specs=[pl.BlockSpec((tm, d), lambda i, j: (i, 0)),
                  pl.BlockSpec((1, d), lambda i, j: (0, 0)),
                  pl.BlockSpec((d, COL_BLOCK), lambda i, j: (0, j))],
        out_specs=pl.BlockSpec((tm, COL_BLOCK), lambda i, j: (i, j)),
        out_shape=jax.ShapeDtypeStruct((n, nu), F32),
        scratch_shapes=[pltpu.VMEM((tm, d), BF16)],
        compiler_params=_params("parallel", "arbitrary"),
        name="in_proj",
    )(x, g.reshape(1, d), w_packed)


def _head_norm_kernel(x_ref, g_ref, o_ref):
    g = g_ref[...]
    for h in range(N_HEADS):
        sl = slice(h * HEAD_DIM, (h + 1) * HEAD_DIM)
        o_ref[:, sl] = _head_rms(x_ref[:, sl], g)


def _head_norm(u, col_block, g, tm):
    n = u.shape[0]
    return pl.pallas_call(
        _head_norm_kernel,
        grid=(n // tm,),
        in_specs=[pl.BlockSpec((tm, COL_BLOCK), lambda i: (i, col_block)),
                  pl.BlockSpec((1, HEAD_DIM), lambda i: (0, 0))],
        out_specs=pl.BlockSpec((tm, COL_BLOCK), lambda i: (i, 0)),
        out_shape=jax.ShapeDtypeStruct((n, COL_BLOCK), F32),
        compiler_params=_params("parallel"),
        name="head_norm",
    )(u, g.reshape(1, HEAD_DIM))


LOG2E = 1.4426950408889634


def _neg_suffix_ones(n, extra_cols=0):
    r = lax.broadcasted_iota(jnp.int32, (n, n + extra_cols), 0)
    c = lax.broadcasted_iota(jnp.int32, (n, n + extra_cols), 1)
    return jnp.where((r > c) | (c >= n), -1.0, 0.0).astype(BF16)


def _softplus2(z2):
    return jnp.maximum(z2, 0.0) + jnp.log2(1.0 + jnp.exp2(-jnp.abs(z2)))


def _query(q, gain):
    return (_head_rms(q, gain) * (QK_SCALE * LOG2E)).astype(BF16)


def _sb_tile(qb, kt, vt, bias2, neg_after, carry, vis):
    z2 = _dot_nt(qb, kt) + bias2
    sp2 = _softplus2(z2)
    skip = sp2 if vis is None else jnp.where(vis, sp2, 0.0)
    skip_b = skip.astype(BF16)
    cu = neg_after.shape[0]
    between = []
    for ch in reversed(range(kt.shape[0] // cu)):
        cols = slice(ch * cu, (ch + 1) * cu)
        between.insert(0, _dot(skip_b[:, cols], neg_after) + carry)
        carry = carry - jnp.sum(skip[:, cols], axis=-1, keepdims=True)
    between = between[0] if len(between) == 1 else jnp.concatenate(between, axis=1)
    w = jnp.exp2(z2 - sp2 + between)
    if vis is not None:
        w = jnp.where(vis, w, 0.0)
    return _dot(w.astype(BF16), vt), carry


def _attn_prompt_kernel(q_ref, k_ref, v_ref, qg_ref, bias_ref, o_ref, qb_ref, acc_ref, car_ref, *, bq, bk, sub):
    i = pl.program_id(2)
    qb_ref[...] = _query(q_ref[...], qg_ref[...])
    bias2 = bias_ref[:, 0:1]
    neg_after = _neg_suffix_ones(sub)
    acc_ref[...] = jnp.zeros_like(acc_ref)
    car_ref[...] = jnp.zeros_like(car_ref)

    def key_block(start, row0, vis):
        kt = k_ref[pl.ds(start, bk), :].astype(BF16)
        vt = v_ref[pl.ds(start, bk), :].astype(BF16)
        rows = slice(row0, bq)
        pv, carry = _sb_tile(qb_ref[rows, :], kt, vt, bias2, neg_after, car_ref[rows, :], vis)
        acc_ref[rows, :] += pv
        car_ref[rows, :] = carry

    for d in reversed(range(bq // bk)):
        r = lax.broadcasted_iota(jnp.int32, (bq - d * bk, bk), 0)
        c = lax.broadcasted_iota(jnp.int32, (bq - d * bk, bk), 1)
        key_block(pl.multiple_of(i * bq + d * bk, bk), d * bk, c < r)

    def body(it, _):
        key_block(pl.multiple_of((i * (bq // bk) - 1 - it) * bk, bk), 0, None)
        return 0

    lax.fori_loop(0, i * (bq // bk), body, 0)
    o_ref[...] = acc_ref[...]


def _attn_prompt(u, kn, qg, bias2_vec, n_batch, seq, bq, bk, sub):
    nq = seq // bq
    hb = COL_BLOCK // HEAD_DIM
    return pl.pallas_call(
        functools.partial(_attn_prompt_kernel, bq=bq, bk=bk, sub=sub),
        grid=(n_batch, N_HEADS, nq),
        in_specs=[pl.BlockSpec((bq, HEAD_DIM), lambda b, h, i: (b * nq + i, CB_AQ * hb + h)),
                  pl.BlockSpec((seq, HEAD_DIM), lambda b, h, i: (b, h)),
                  pl.BlockSpec((seq, HEAD_DIM), lambda b, h, i: (b, CB_AV * hb + h)),
                  pl.BlockSpec((1, HEAD_DIM), lambda b, h, i: (0, 0)),
                  pl.BlockSpec((1, HEAD_DIM), lambda b, h, i: (0, h))],
        out_specs=pl.BlockSpec((bq, HEAD_DIM), lambda b, h, i: (b * nq + i, h)),
        out_shape=jax.ShapeDtypeStruct((n_batch * seq, COL_BLOCK), F32),
        scratch_shapes=[pltpu.VMEM((bq, HEAD_DIM), BF16), pltpu.VMEM((bq, HEAD_DIM), F32),
                        pltpu.VMEM((bq, 1), F32)],
        compiler_params=_params("parallel", "parallel", "arbitrary"),
        name="attn_prompt",
    )(u, kn, u, qg.reshape(1, HEAD_DIM), bias2_vec)


def _attn_sample_kernel(pt_ref, q_ref, kn_ref, vn_ref, qg_ref, bias_ref, *rest, n_new, pages_per_step):
    page_refs = rest[:2 * pages_per_step]
    o_ref, acc_ref, car_ref, knp_ref, vnp_ref = rest[2 * pages_per_step:]
    j = pl.program_id(1)
    qg = qg_ref[...]
    neg_after = _neg_suffix_ones(PAGE_SIZE, LANE)
    head = lambda h: slice(h * HEAD_DIM, (h + 1) * HEAD_DIM)
    qbs = [_query(q_ref[:, head(h)], qg) for h in range(N_HEADS)]

    def tiles(keys, values, vis, carries, accs):
        n_blocks = len(keys)
        z2 = [[_dot_nt(qbs[h], keys[p][h]) + bias_ref[:, head(h)] for h in range(N_HEADS)] for p in range(n_blocks)]
        sp2 = [[_softplus2(z) for z in zs] for zs in z2]
        skip = [[s if vis is None else jnp.where(vis, s, 0.0) for s in ss] for ss in sp2]
        stacked = jnp.concatenate([s for ss in skip for s in ss], axis=0)
        sums = _dot(stacked.astype(BF16), neg_after)
        carries, accs = list(carries), list(accs)
        for p in range(n_blocks):
            for h in range(N_HEADS):
                row0 = (p * N_HEADS + h) * n_new
                blk = sums[row0:row0 + n_new]
                w = jnp.exp2(z2[p][h] - sp2[p][h] + blk[:, :PAGE_SIZE] + carries[h])
                if vis is not None:
                    w = jnp.where(vis, w, 0.0)
                accs[h] = accs[h] + _dot(w.astype(BF16), values[p][h])
                carries[h] = carries[h] + blk[:, PAGE_SIZE:]
        return carries, accs

    @pl.when(j == 0)
    def _():
        knp_ref[...] = jnp.zeros_like(knp_ref)
        vnp_ref[...] = jnp.zeros_like(vnp_ref)
        knp_ref[0:n_new, :] = kn_ref[...]
        vnp_ref[0:n_new, :] = vn_ref[...]
        r = lax.broadcasted_iota(jnp.int32, (n_new, PAGE_SIZE), 0)
        c = lax.broadcasted_iota(jnp.int32, (n_new, PAGE_SIZE), 1)
        zero = jnp.zeros((n_new, HEAD_DIM), F32)
        carries, accs = tiles([[knp_ref[:, head(h)].astype(BF16) for h in range(N_HEADS)]],
                              [[vnp_ref[:, head(h)].astype(BF16) for h in range(N_HEADS)]],
                              c < r, [zero] * N_HEADS, [zero] * N_HEADS)
        for h in range(N_HEADS):
            car_ref[:, head(h)] = carries[h]
            acc_ref[:, head(h)] = accs[h]

    page = lambda ref, h: ref[0, pl.ds(h, PAGE_SIZE, stride=N_HEADS), :].astype(BF16)
    keys = [[page(page_refs[p], h) for h in range(N_HEADS)] for p in range(pages_per_step)]
    values = [[page(page_refs[pages_per_step + p], h) for h in range(N_HEADS)] for p in range(pages_per_step)]
    carries, accs = tiles(keys, values, None, [car_ref[:, head(h)] for h in range(N_HEADS)],
                          [acc_ref[:, head(h)] for h in range(N_HEADS)])
    for h in range(N_HEADS):
        car_ref[:, head(h)] = carries[h]
        acc_ref[:, head(h)] = accs[h]
        o_ref[:, head(h)] = accs[h]


def _attn_sample(u, kn, cache_k, cache_v, page_table, layer, qg, bias2_vec, n_new, pages_per_step):
    n_dec, n_pages = page_table.shape
    depth, n_pool = cache_k.shape[0], cache_k.shape[1]
    ck = cache_k.reshape(depth * n_pool, PAGE_SIZE * N_HEADS, HEAD_DIM)
    cv = cache_v.reshape(depth * n_pool, PAGE_SIZE * N_HEADS, HEAD_DIM)
    n_steps = n_pages // pages_per_step
    base = layer * n_pool

    def page_map(p):
        def index(b, j, pt):
            logical = (n_steps - 1 - j) * pages_per_step + (pages_per_step - 1 - p)
            return (base + pt[b * n_pages + logical], 0, 0)
        return index

    page_specs = [pl.BlockSpec((1, PAGE_SIZE * N_HEADS, HEAD_DIM), page_map(p)) for p in range(pages_per_step)]
    grid_spec = pltpu.PrefetchScalarGridSpec(
        num_scalar_prefetch=1,
        grid=(n_dec, n_steps),
        in_specs=[pl.BlockSpec((n_new, COL_BLOCK), lambda b, j, pt: (b, CB_AQ)),
                  pl.BlockSpec((n_new, COL_BLOCK), lambda b, j, pt: (b, 0)),
                  pl.BlockSpec((n_new, COL_BLOCK), lambda b, j, pt: (b, CB_AV)),
                  pl.BlockSpec((1, HEAD_DIM), lambda b, j, pt: (0, 0)),
                  pl.BlockSpec((1, COL_BLOCK), lambda b, j, pt: (0, 0))] + page_specs + page_specs,
        out_specs=pl.BlockSpec((n_new, COL_BLOCK), lambda b, j, pt: (b, 0)),
        scratch_shapes=[pltpu.VMEM((n_new, COL_BLOCK), F32), pltpu.VMEM((n_new, COL_BLOCK), F32),
                        pltpu.VMEM((PAGE_SIZE, COL_BLOCK), F32), pltpu.VMEM((PAGE_SIZE, COL_BLOCK), F32)],
    )
    return pl.pallas_call(
        functools.partial(_attn_sample_kernel, n_new=n_new, pages_per_step=pages_per_step),
        grid_spec=grid_spec,
        out_shape=jax.ShapeDtypeStruct((n_dec * n_new, COL_BLOCK), F32),
        compiler_params=_params("parallel", "arbitrary"),
        name="attn_sample",
    )(page_table.reshape(-1), u, kn, u, qg.reshape(1, HEAD_DIM), bias2_vec,
      *([ck] * pages_per_step), *([cv] * pages_per_step))


POOL_HALO = 16


def _pool_kernel(x_ref, halo_ref, w_ref, scale_ref, o_ref, buf_ref, *, tm, seq_rows, prefix_rows, pos0):
    i = pl.program_id(0)
    buf_ref[0:POOL_HALO, :] = halo_ref[...]
    buf_ref[POOL_HALO:, :] = x_ref[...]
    row = lax.broadcasted_iota(jnp.int32, (tm, LANE), 0) + i * tm
    pos = row & (seq_rows - 1)
    abs_pos = pos + (pos0 - prefix_rows)
    for g, win in enumerate(POOL_WINDOWS):
        sl = slice(g * LANE, (g + 1) * LANE)
        x = x_ref[:, sl]
        total = x
        for s in range(1, win):
            shifted = buf_ref[POOL_HALO - s:POOL_HALO - s + tm, sl]
            total = total + jnp.where(pos >= s, shifted, 0.0)
        count = jnp.minimum(abs_pos + 1, win).astype(F32)
        pooled = total / count - x
        mixed = _dot(pooled.astype(BF16), w_ref[g].astype(BF16))
        o_ref[:, sl] = mixed * scale_ref[:, sl]


def _pool(x_arr, col_block, pool_w, pool_scale, tm, seq_rows, prefix_rows, pos0):
    n = x_arr.shape[0]
    hb = tm // POOL_HALO
    return pl.pallas_call(
        functools.partial(_pool_kernel, tm=tm, seq_rows=seq_rows, prefix_rows=prefix_rows, pos0=pos0),
        grid=(n // tm,),
        in_specs=[pl.BlockSpec((tm, COL_BLOCK), lambda i: (i, col_block)),
                  pl.BlockSpec((POOL_HALO, COL_BLOCK), lambda i: (jnp.maximum(i * hb - 1, 0), col_block)),
                  pl.BlockSpec((len(POOL_WINDOWS), LANE, LANE), lambda i: (0, 0, 0)),
                  pl.BlockSpec((1, COL_BLOCK), lambda i: (0, 0))],
        out_specs=pl.BlockSpec((tm, COL_BLOCK), lambda i: (i, 0)),
        out_shape=jax.ShapeDtypeStruct((n, COL_BLOCK), F32),
        scratch_shapes=[pltpu.VMEM((tm + POOL_HALO, COL_BLOCK), F32)],
        compiler_params=_params("parallel"),
        name="pool",
    )(x_arr, x_arr, pool_w, pool_scale.reshape(1, COL_BLOCK))


def _rotary(x, cos, sin_signed):
    return x * cos + pltpu.roll(x, HEAD_DIM // 2, 1) * sin_signed


def _retention_kernel(q_ref, k_ref, v_ref, cos_ref, sin_ref, din_ref, dq_ref, dk_ref, dc_ref, ng_ref, *rest,
                      chunk, tiles_per_seq):
    carried = tiles_per_seq > 0
    if carried:
        o_ref, sout_ref, s_ref = rest
    else:
        s0_ref, o_ref, sout_ref = rest
    i = pl.program_id(0)
    n_chunks = ROW_TILE // chunk
    cos, sin = cos_ref[...], sin_ref[...]
    ng = ng_ref[...]
    if carried:
        @pl.when(i % tiles_per_seq == 0)
        def _():
            s_ref[...] = jnp.zeros_like(s_ref)
    row = lax.broadcasted_iota(jnp.int32, (ROW_TILE, LANE), 0)

    heads = range(N_HEADS)
    head = lambda h: slice(h * HEAD_DIM, (h + 1) * HEAD_DIM)
    qr = [(_rotary(q_ref[:, head(h)], cos, sin) * QK_SCALE).astype(BF16) for h in heads]
    kr = [_rotary(k_ref[:, head(h)], cos, sin) for h in heads]
    vb = [v_ref[:, head(h)].astype(BF16) for h in heads]
    s = [(_dot_nt(qr[h], kr[h].astype(BF16)) * din_ref[h]).astype(BF16) for h in heads]
    kd = [kr[h] * dk_ref[h] for h in heads]
    if carried:
        state = [s_ref[h] for h in heads]
        inter = [_dot(qr[h], state[h].astype(BF16)) for h in heads]
        new_state = [state[h] * dc_ref[h] + _dot_tn(kd[h].astype(BF16), vb[h]) for h in heads]
        for h in heads:
            s_ref[h] = new_state[h]

        @pl.when(i % tiles_per_seq == tiles_per_seq - 1)
        def _():
            for h in heads:
                sout_ref[0, h] = new_state[h]
    else:
        chunks = range(n_chunks)
        rows = lambda c: slice(c * chunk, (c + 1) * chunk)
        inter = [jnp.concatenate([_dot(qr[h][rows(c)], s0_ref[c, h].astype(BF16)) for c in chunks], axis=0)
                 for h in heads]
        for h in heads:
            for c in chunks:
                in_chunk = (row >= c * chunk) & (row < (c + 1) * chunk)
                kd_c = jnp.where(in_chunk, kd[h], 0.0).astype(BF16)
                sout_ref[c, h] = s0_ref[c, h] * dc_ref[h] + _dot_tn(kd_c, vb[h])
    for h in heads:
        o = _dot(s[h], vb[h]) + inter[h] * dq_ref[h]
        o_ref[:, head(h)] = _head_rms(o, ng)


def _retention_tables(chunk, pos_rows):
    heads = jnp.arange(N_HEADS, dtype=F32)
    log_gamma = jnp.log(1.0 - 2.0 ** (-5.0 - heads))
    r = jnp.arange(ROW_TILE)
    i = (r % chunk).astype(F32)
    diff = i[:, None] - i[None, :]
    same = (r[:, None] // chunk) == (r[None, :] // chunk)
    dec_in = jnp.where(same & (diff >= 0), jnp.exp(log_gamma[:, None, None] * jnp.maximum(diff, 0.0)), 0.0)
    dec_q = jnp.exp(log_gamma[:, None] * (i + 1.0))
    dec_k = jnp.exp(log_gamma[:, None] * (chunk - 1.0 - i))
    dec_c = jnp.exp(log_gamma * chunk)
    lanes = (N_HEADS, ROW_TILE, LANE)
    half = HEAD_DIM // 2
    inv_freq = ROPE_BASE ** (-jnp.arange(half, dtype=F32) / half)
    ang = pos_rows.astype(F32)[:, None] * inv_freq[None, :]
    cos = jnp.concatenate([jnp.cos(ang), jnp.cos(ang)], axis=-1)
    sin = jnp.concatenate([-jnp.sin(ang), jnp.sin(ang)], axis=-1)
    return (cos, sin, dec_in, jnp.broadcast_to(dec_q[:, :, None], lanes), jnp.broadcast_to(dec_k[:, :, None], lanes),
            jnp.broadcast_to(dec_c[:, None, None], (N_HEADS, 1, LANE)))


def _retention(u, norm_g, chunk, tiles_per_seq, pos_rows, s0=None):
    n = u.shape[0]
    n_tiles = n // ROW_TILE
    carried = tiles_per_seq > 0
    cos, sin, dec_in, dec_q, dec_k, dec_c = _retention_tables(chunk, pos_rows)
    rope_tiles = cos.shape[0] // ROW_TILE
    blk = lambda cb: pl.BlockSpec((ROW_TILE, COL_BLOCK), lambda i: (i, cb))
    rope = pl.BlockSpec((ROW_TILE, HEAD_DIM), lambda i: (i % rope_tiles, 0))
    table = pl.BlockSpec((N_HEADS, ROW_TILE, LANE), lambda i: (0, 0, 0))
    in_specs = [blk(CB_CQ), blk(CB_CK), blk(CB_CV), rope, rope, table, table, table,
                pl.BlockSpec((N_HEADS, 1, LANE), lambda i: (0, 0, 0)),
                pl.BlockSpec((1, HEAD_DIM), lambda i: (0, 0))]
    args = [u, u, u, cos, sin, dec_in, dec_q, dec_k, dec_c, norm_g.reshape(1, HEAD_DIM)]
    if carried:
        n_seq = n_tiles // tiles_per_seq
        state_spec = pl.BlockSpec((1, N_HEADS, HEAD_DIM, HEAD_DIM), lambda i: (i // tiles_per_seq, 0, 0, 0))
        scratch = [pltpu.VMEM((N_HEADS, HEAD_DIM, HEAD_DIM), F32)]
    else:
        n_chunks = ROW_TILE // chunk
        n_seq = n_tiles * n_chunks
        state_spec = pl.BlockSpec((n_chunks, N_HEADS, HEAD_DIM, HEAD_DIM), lambda i: (i, 0, 0, 0))
        in_specs.append(state_spec)
        args.append(s0)
        scratch = []
    return pl.pallas_call(
        functools.partial(_retention_kernel, chunk=chunk, tiles_per_seq=tiles_per_seq),
        grid=(n_tiles,),
        in_specs=in_specs,
        out_specs=[pl.BlockSpec((ROW_TILE, COL_BLOCK), lambda i: (i, 0)), state_spec],
        out_shape=[jax.ShapeDtypeStruct((n, COL_BLOCK), F32),
                   jax.ShapeDtypeStruct((n_seq, N_HEADS, HEAD_DIM, HEAD_DIM), F32)],
        scratch_shapes=scratch,
        compiler_params=_params("arbitrary"),
        name="retention",
    )(*args)


CONV_HALO = 8


def _delta_kernel(xq_ref, xk_ref, xv_ref, hq_ref, hk_ref, hv_ref, ab_ref, cw_ref, alog_ref, dt_ref, ng_ref, *rest,
                  chunk, tiles_per_seq, seq_rows, prefix_rows):
    carried = tiles_per_seq > 0
    if carried:
        o_ref, sout_ref, s_ref, buf_ref = rest
    else:
        s0_ref, o_ref, sout_ref, buf_ref = rest
    i = pl.program_id(0)
    n_chunks = ROW_TILE // chunk
    ng = ng_ref[...]
    if carried:
        @pl.when(i % tiles_per_seq == 0)
        def _():
            s_ref[...] = jnp.zeros_like(s_ref)

    row = lax.broadcasted_iota(jnp.int32, (ROW_TILE, LANE), 0)
    col = lax.broadcasted_iota(jnp.int32, (ROW_TILE, LANE), 1)
    pos = (row + i * ROW_TILE) & (seq_rows - 1)
    real = pos >= prefix_rows
    shift = int(np.log2(chunk))
    same = (row >> shift) == (col >> shift)
    incl = same & (row >= col)
    strict = same & (row > col)
    incl_ones = jnp.where(incl, 1.0, 0.0).astype(BF16)
    same_ones = jnp.where(same, 1.0, 0.0).astype(BF16)

    ab = ab_ref[...]
    g = jnp.where(real, -jnp.exp(alog_ref[...]) * _softplus(ab + dt_ref[...]), 0.0)
    beta_all = jnp.where(real, jax.nn.sigmoid(ab), 0.0)
    g3 = _split3(g)
    gcum = sum(_dot(incl_ones, p) for p in g3)
    glast = sum(_dot(same_ones, p) for p in g3)
    gcum_t = gcum.T
    egc = jnp.exp(gcum)
    erest = jnp.exp(glast - gcum)
    elast = jnp.exp(glast)

    def conv(x_ref, halo_ref, lo):
        buf_ref[0:CONV_HALO, :] = halo_ref[...]
        buf_ref[CONV_HALO:, :] = x_ref[...]
        outs = []
        for h in range(N_HEADS):
            sl = slice(h * HEAD_DIM, (h + 1) * HEAD_DIM)
            wsl = slice(lo + h * HEAD_DIM, lo + (h + 1) * HEAD_DIM)
            out = x_ref[:, sl] * cw_ref[CONV_WIDTH - 1:CONV_WIDTH, wsl]
            for s in range(1, CONV_WIDTH):
                shifted = buf_ref[CONV_HALO - s:CONV_HALO - s + ROW_TILE, sl]
                out = out + jnp.where(pos >= s, shifted, 0.0) * cw_ref[CONV_WIDTH - 1 - s:CONV_WIDTH - s, wsl]
            outs.append(_silu(out))
        return outs

    cq = conv(xq_ref, hq_ref, 0)
    ck = conv(xk_ref, hk_ref, COL_BLOCK)
    cv = conv(xv_ref, hv_ref, 2 * COL_BLOCK)

    heads = range(N_HEADS)
    col = lambda m, h: m[:, h:h + 1]
    l2 = lambda x: x * lax.rsqrt(jnp.sum(x * x, axis=-1, keepdims=True) + EPS)
    qn = [l2(cq[h]) * QK_SCALE for h in heads]
    kn = [jnp.where(real, l2(ck[h]), 0.0) for h in heads]
    beta = [col(beta_all, N_HEADS + h) for h in heads]
    decay = [jnp.where(incl, jnp.exp(jnp.where(incl, col(gcum, h) - gcum_t[h:h + 1, :], 0.0)), 0.0) for h in heads]
    kbeta = [kn[h] * beta[h] for h in heads]
    knb = [kn[h].astype(BF16) for h in heads]
    a_mat = [jnp.where(strict, _dot_nt(kbeta[h].astype(BF16), knb[h]) * decay[h], 0.0) for h in heads]
    y = [-a for a in a_mat]
    p2 = [_split2(a) for a in a_mat]
    for _ in range(int(np.log2(chunk)) - 1):
        p = [_dot_split(p2[h], p2[h]) for h in heads]
        p2 = [_split2(p[h]) for h in heads]
        y = [y[h] + p[h] + _dot_split(_split2(y[h]), p2[h]) for h in heads]
    rhs = [jnp.concatenate([cv[h] * beta[h], kbeta[h] * col(egc, h)], axis=1) for h in heads]
    sol = [rhs[h] + _dot(y[h].astype(BF16), rhs[h].astype(BF16)) for h in heads]
    u_mat = [s[:, :HEAD_DIM] for s in sol]
    wb = [s[:, HEAD_DIM:].astype(BF16) for s in sol]
    attn = [(_dot_nt(qn[h].astype(BF16), knb[h]) * decay[h]).astype(BF16) for h in heads]
    qg = [(qn[h] * col(egc, h)).astype(BF16) for h in heads]
    kdec = [kn[h] * col(erest, h) for h in heads]
    if carried:
        state = [s_ref[h] for h in heads]
        sb = [st.astype(BF16) for st in state]
        vnb = [(u_mat[h] - _dot(wb[h], sb[h])).astype(BF16) for h in heads]
        o = [_dot(qg[h], sb[h]) + _dot(attn[h], vnb[h]) for h in heads]
        new_state = [state[h] * elast[0:1, h:h + 1] + _dot_tn(kdec[h].astype(BF16), vnb[h]) for h in heads]
        for h in heads:
            s_ref[h] = new_state[h]

        @pl.when(i % tiles_per_seq == tiles_per_seq - 1)
        def _():
            for h in heads:
                sout_ref[0, h] = new_state[h]
    else:
        chunks = range(n_chunks)
        rows = lambda c: slice(c * chunk, (c + 1) * chunk)
        sb = [[s0_ref[c, h].astype(BF16) for c in chunks] for h in heads]
        ws = [jnp.concatenate([_dot(wb[h][rows(c)], sb[h][c]) for c in chunks], axis=0) for h in heads]
        vnb = [(u_mat[h] - ws[h]).astype(BF16) for h in heads]
        qs = [jnp.concatenate([_dot(qg[h][rows(c)], sb[h][c]) for c in chunks], axis=0) for h in heads]
        o = [qs[h] + _dot(attn[h], vnb[h]) for h in heads]
        for h in heads:
            for c in chunks:
                in_chunk = (row >= c * chunk) & (row < (c + 1) * chunk)
                kdec_c = jnp.where(in_chunk, kdec[h], 0.0).astype(BF16)
                sout_ref[c, h] = s0_ref[c, h] * elast[c * chunk:c * chunk + 1, h:h + 1] + _dot_tn(kdec_c, vnb[h])
    for h in heads:
        o_ref[:, h * HEAD_DIM:(h + 1) * HEAD_DIM] = _head_rms(o[h], ng)


def _delta(x_arr, cb_q, ab_arr, ab_block, conv_w, a_log, dt_bias, norm_g, chunk, tiles_per_seq, seq_rows,
           prefix_rows, s0=None):
    n = x_arr.shape[0]
    n_tiles = n // ROW_TILE
    carried = tiles_per_seq > 0
    hb = ROW_TILE // CONV_HALO
    blk = lambda cb: pl.BlockSpec((ROW_TILE, COL_BLOCK), lambda i: (i, cb))
    halo = lambda cb: pl.BlockSpec((CONV_HALO, COL_BLOCK), lambda i: (jnp.maximum(i * hb - 1, 0), cb))
    vec = pl.BlockSpec((1, LANE), lambda i: (0, 0))
    pad = lambda v: jnp.zeros((1, LANE), F32).at[0, :N_HEADS].set(v)
    in_specs = [blk(cb_q), blk(cb_q + 1), blk(cb_q + 2), halo(cb_q), halo(cb_q + 1), halo(cb_q + 2),
                pl.BlockSpec((ROW_TILE, LANE), lambda i: (i, ab_block)),
                pl.BlockSpec((CONV_WIDTH, 3 * COL_BLOCK), lambda i: (0, 0)), vec, vec,
                pl.BlockSpec((1, HEAD_DIM), lambda i: (0, 0))]
    args = [x_arr] * 6 + [ab_arr, conv_w, pad(a_log), pad(dt_bias), norm_g.reshape(1, HEAD_DIM)]
    scratch = [pltpu.VMEM((ROW_TILE + CONV_HALO, COL_BLOCK), F32)]
    if carried:
        n_seq = n_tiles // tiles_per_seq
        state_spec = pl.BlockSpec((1, N_HEADS, HEAD_DIM, HEAD_DIM), lambda i: (i // tiles_per_seq, 0, 0, 0))
        scratch = [pltpu.VMEM((N_HEADS, HEAD_DIM, HEAD_DIM), F32)] + scratch
    else:
        n_chunks = ROW_TILE // chunk
        n_seq = n_tiles * n_chunks
        state_spec = pl.BlockSpec((n_chunks, N_HEADS, HEAD_DIM, HEAD_DIM), lambda i: (i, 0, 0, 0))
        in_specs.append(state_spec)
        args.append(s0)
    return pl.pallas_call(
        functools.partial(_delta_kernel, chunk=chunk, tiles_per_seq=tiles_per_seq, seq_rows=seq_rows,
                          prefix_rows=prefix_rows),
        grid=(n_tiles,),
        in_specs=in_specs,
        out_specs=[pl.BlockSpec((ROW_TILE, COL_BLOCK), lambda i: (i, 0)), state_spec],
        out_shape=[jax.ShapeDtypeStruct((n, COL_BLOCK), F32),
                   jax.ShapeDtypeStruct((n_seq, N_HEADS, HEAD_DIM, HEAD_DIM), F32)],
        scratch_shapes=scratch,
        compiler_params=_params("arbitrary"),
        name="delta",
    )(*args)


def _out_proj_kernel(x_ref, oa_ref, ob_ref, oc_ref, od_ref, ga_ref, gb_ref, gc_ref, gd_ref,
                     ma_ref, mb_ref, mc_ref, md_ref, wb_ref, wo_ref, y_ref):
    mixed = None
    branches = ((oa_ref, ga_ref, ma_ref), (ob_ref, gb_ref, mb_ref), (oc_ref, gc_ref, mc_ref), (od_ref, gd_ref, md_ref))
    for n, (o_ref, g_ref, m_ref) in enumerate(branches):
        branch = (o_ref[...] * _silu(g_ref[...])).astype(BF16)
        proj = _dot(branch, wb_ref[n])
        gated = jax.nn.sigmoid(m_ref[...]) * proj
        mixed = gated if mixed is None else mixed + gated
    y_ref[...] = x_ref[...] + _dot(mixed.astype(BF16), wo_ref[...])


def _out_proj(x, u, oa, ob, oc, od, w_branch, w_out, tm):
    n = x.shape[0]
    row = lambda width, cb: pl.BlockSpec((tm, width), lambda i: (i, cb))
    merge0 = CB_MERGE * COL_BLOCK // D_MODEL
    return pl.pallas_call(
        _out_proj_kernel,
        grid=(n // tm,),
        in_specs=[row(D_MODEL, 0)] + [row(COL_BLOCK, 0)] * 4
                 + [row(COL_BLOCK, CB_AG), row(COL_BLOCK, CB_BG), row(COL_BLOCK, CB_CG), row(COL_BLOCK, CB_DG)]
                 + [row(D_MODEL, merge0 + b) for b in range(N_BRANCH)]
                 + [pl.BlockSpec((N_BRANCH, BRANCH_WIDTH, D_MODEL), lambda i: (0, 0, 0)),
                    pl.BlockSpec((D_MODEL, D_MODEL), lambda i: (0, 0))],
        out_specs=row(D_MODEL, 0),
        out_shape=jax.ShapeDtypeStruct((n, D_MODEL), F32),
        compiler_params=_params("parallel"),
        name="out_proj",
    )(x, oa, ob, oc, od, *([u] * 8), w_branch, w_out)


def _kv_leaves_kernel(*refs, tm):
    ko_ref, vo_ref = refs[-2:]
    layer = pl.program_id(0)
    for li in range(len(refs) // 2 - 1):
        @pl.when(layer == li)
        def _(k_ref=refs[2 * li], v_ref=refs[2 * li + 1]):
            for h in range(N_HEADS):
                rows = pl.ds(h, tm, stride=N_HEADS)
                ko_ref[0, rows, :] = k_ref[:, h * HEAD_DIM:(h + 1) * HEAD_DIM]
                vo_ref[0, rows, :] = v_ref[:, h * HEAD_DIM:(h + 1) * HEAD_DIM]


def _kv_leaves(kn_and_u, tm):
    depth = len(kn_and_u) // 2
    n = kn_and_u[0].shape[0]
    spec = lambda li, cb: pl.BlockSpec((tm, COL_BLOCK), lambda l, i: (jnp.where(l == li, i, 0), cb))
    in_specs = [spec(li, cb) for li in range(depth) for cb in (0, CB_AV)]
    out_spec = pl.BlockSpec((1, tm * N_HEADS, HEAD_DIM), lambda l, i: (l, i, 0))
    out_shape = jax.ShapeDtypeStruct((depth, n * N_HEADS, HEAD_DIM), F32)
    return pl.pallas_call(
        functools.partial(_kv_leaves_kernel, tm=tm),
        grid=(depth, n // tm),
        in_specs=in_specs,
        out_specs=[out_spec, out_spec],
        out_shape=[out_shape, out_shape],
        compiler_params=_params("arbitrary", "arbitrary"),
        name="kv_leaves",
    )(*kn_and_u)


def _pack_w_in(w):
    d_ab = 13 * COL_BLOCK
    tail = jnp.zeros((w.shape[0], COL_BLOCK - 2 * N_HEADS), w.dtype)
    return jnp.concatenate([w[:, :d_ab], w[:, d_ab + 2 * N_HEADS:], w[:, d_ab:d_ab + 2 * N_HEADS], tail],
                           axis=1).astype(BF16)


def _pad_front(hist, x, seq_rows):
    n, t, w = x.shape
    p = hist.shape[1]
    zeros = jnp.zeros((n, seq_rows - t - p, w), x.dtype)
    return jnp.concatenate([zeros, hist, x], axis=1).reshape(n * seq_rows, w)


def kernel(x_prompt, x_sample, cache_k, cache_v, state_pool, state_ret, state_delta, state_conv, page_table,
           norm_g, w_in, sb_q_norm, sb_k_norm, sb_bias, pool_w, pool_scale, ret_norm,
           dn_conv_w, dn_a_log, dn_dt_bias, dn_norm, w_branch, w_out):
    n_batch, seq, _ = x_prompt.shape
    n_dec, n_new, _ = x_sample.shape
    depth = norm_g.shape[0]
    past = page_table.shape[1] * PAGE_SIZE
    heads = (N_HEADS, HEAD_DIM)
    pool_rows, delta_rows = 32, 16
    assert (n_dec * n_new) % (2 * ROW_TILE) == 0 and seq % 1024 == 0 and past >= max(POOL_WINDOWS)

    xp = x_prompt.reshape(n_batch * seq, D_MODEL)
    xs = x_sample.reshape(n_dec * n_new, D_MODEL)
    pos_p = jnp.arange(seq)
    pos_s = past + (jnp.arange(ROW_TILE) % n_new)
    new_p, new_s, kv_p = [], [], []
    for l in range(depth):
        w_packed = _pack_w_in(w_in[l])
        wb = w_branch[l].astype(BF16)
        wo = w_out[l].astype(BF16)
        bias2_vec = jnp.repeat(sb_bias[l] * LOG2E, HEAD_DIM).reshape(1, COL_BLOCK)
        ab_block = CB_DAB * COL_BLOCK // LANE

        u = _in_proj(xp, norm_g[l], w_packed, 1024)
        kn = _head_norm(u, CB_AK, sb_k_norm[l], 1024)
        oa = _attn_prompt(u, kn, sb_q_norm[l], bias2_vec, n_batch, seq, 512, 512, 256)
        ob = _pool(u, CB_BIN, pool_w[l], pool_scale[l], 512, seq, 0, 0)
        oc, ret = _retention(u, ret_norm[l], ROW_TILE, seq // ROW_TILE, pos_p)
        od, delta = _delta(u, CB_DQ, u, ab_block, dn_conv_w[l], dn_a_log[l], dn_dt_bias[l], dn_norm[l],
                           ROW_TILE, seq // ROW_TILE, seq, 0)
        xp = _out_proj(xp, u, oa, ob, oc, od, wb, wo, 256)
        last = lambda rows, cb, width: u.reshape(n_batch, seq, -1)[:, seq - rows:, cb * COL_BLOCK:cb * COL_BLOCK + width]
        kv_p += [kn, u]
        new_p.append((last(POOL_STATE, CB_BIN, COL_BLOCK), ret, delta, last(CONV_WIDTH - 1, CB_DQ, 3 * COL_BLOCK)))

        u = _in_proj(xs, norm_g[l], w_packed, n_dec * n_new)
        kn = _head_norm(u, CB_AK, sb_k_norm[l], n_dec * n_new)
        oa = _attn_sample(u, kn, cache_k, cache_v, page_table, l, sb_q_norm[l], bias2_vec, n_new, 16)
        cols = lambda cb, width=COL_BLOCK: u[:, cb * COL_BLOCK:cb * COL_BLOCK + width].reshape(n_dec, n_new, width)
        b_in, d_qkv = cols(CB_BIN), cols(CB_DQ, 3 * COL_BLOCK)
        ob = _pool(_pad_front(state_pool[l], b_in, pool_rows), 0, pool_w[l], pool_scale[l], 256, pool_rows,
                   pool_rows - n_new, past)
        ob = ob.reshape(n_dec, pool_rows, COL_BLOCK)[:, -n_new:].reshape(n_dec * n_new, COL_BLOCK)
        oc, ret = _retention(u, ret_norm[l], n_new, 0, pos_s, state_ret[l])
        d_ab = u[:, CB_DAB * COL_BLOCK:CB_DAB * COL_BLOCK + LANE].reshape(n_dec, n_new, LANE)
        od, delta = _delta(_pad_front(state_conv[l], d_qkv, delta_rows), 0,
                           _pad_front(jnp.zeros((n_dec, 0, LANE), F32), d_ab, delta_rows), 0,
                           dn_conv_w[l], dn_a_log[l], dn_dt_bias[l], dn_norm[l],
                           delta_rows, 0, delta_rows, delta_rows - n_new, state_delta[l])
        od = od.reshape(n_dec, delta_rows, COL_BLOCK)[:, -n_new:].reshape(n_dec * n_new, COL_BLOCK)
        xs = _out_proj(xs, u, oa, ob, oc, od, wb, wo, 256)
        new_s.append((kn.reshape(n_dec, n_new, *heads), cols(CB_AV).reshape(n_dec, n_new, *heads),
                      jnp.concatenate([state_pool[l], b_in], axis=1)[:, -POOL_STATE:], ret, delta,
                      jnp.concatenate([state_conv[l], d_qkv], axis=1)[:, -(CONV_WIDTH - 1):]))

    stack = lambda states, i: jnp.stack([s[i] for s in states], axis=0)
    k_prompt, v_prompt = _kv_leaves(kv_p, 512)
    return (xp.reshape(n_batch, seq, D_MODEL), xs.reshape(n_dec, n_new, D_MODEL),
            k_prompt.reshape(depth, n_batch, seq, *heads), v_prompt.reshape(depth, n_batch, seq, *heads),
            *[stack(new_p, i) for i in range(4)], *[stack(new_s, i) for i in range(6)])
```

```python
import functools

import jax
import jax.numpy as jnp
import numpy as np
from jax import lax
from jax.experimental import pallas as pl
from jax.experimental.pallas import tpu as pltpu

F32 = jnp.float32
BF16 = jnp.bfloat16

D_MODEL = 1024
BRANCH_WIDTH = 512
HEAD_DIM = 128
N_HEADS = 4
N_BRANCH = 4
PAGE_SIZE = 128
POOL_WINDOWS = (2, 4, 8, 16)
POOL_STATE = 15
CONV_WIDTH = 4
ROPE_BASE = 10000.0
EPS = 1e-6
QK_SCALE = HEAD_DIM ** -0.5

LANE = 128
ROW_TILE = 128
COL_BLOCK = 512
VMEM_LIMIT = 56 * 1024 * 1024

CB_AQ, CB_AK, CB_AV, CB_AG, CB_BIN, CB_BG, CB_CQ, CB_CK, CB_CV, CB_CG = range(10)
CB_DQ, CB_DK, CB_DV, CB_DG, CB_MERGE, CB_DAB = 10, 11, 12, 13, 14, 22
N_COL_BLOCKS = 23


def _params(*sem):
    return pltpu.CompilerParams(dimension_semantics=sem, vmem_limit_bytes=VMEM_LIMIT)


def _dot(a, b):
    return jnp.dot(a, b, preferred_element_type=F32)


def _dot_nt(a, b):
    return lax.dot_general(a, b, (((1,), (1,)), ((), ())), preferred_element_type=F32)


def _dot_tn(a, b):
    return lax.dot_general(a, b, (((0,), (0,)), ((), ())), preferred_element_type=F32)


def _softplus(x):
    return jnp.maximum(x, 0.0) + jnp.log1p(jnp.exp(-jnp.abs(x)))


def _silu(x):
    return x * jax.nn.sigmoid(x)


def _head_rms(x, g):
    return x * lax.rsqrt(jnp.mean(x * x, axis=-1, keepdims=True) + EPS) * g


def _split2(x):
    hi = x.astype(BF16)
    return hi, (x - hi.astype(F32)).astype(BF16)


def _dot_split(a, b):
    return _dot(a[0], b[0]) + (_dot(a[0], b[1]) + _dot(a[1], b[0]))


def _split3(x):
    a = x.astype(BF16)
    r = x - a.astype(F32)
    b = r.astype(BF16)
    c = (r - b.astype(F32)).astype(BF16)
    return a, b, c


def _in_proj_kernel(x_ref, g_ref, w_ref, kg_ref, o_ref, kn_ref, h_ref):
    j = pl.program_id(1)

    @pl.when(j == 0)
    def _():
        x = x_ref[...]
        h_ref[...] = _head_rms(x, g_ref[...]).astype(BF16)

    out = _dot(h_ref[...], w_ref[0])
    o_ref[...] = out

    @pl.when(j == CB_AK)
    def _():
        kg = kg_ref[...]
        for h in range(N_HEADS):
            sl = slice(h * HEAD_DIM, (h + 1) * HEAD_DIM)
            kn_ref[:, sl] = _head_rms(out[:, sl], kg)


def _in_proj(x, g, w_packed, layer, k_gain, tm):
    n, d = x.shape
    nu = w_packed.shape[2]
    return pl.pallas_call(
        _in_proj_kernel,
        grid=(n // tm, nu // COL_BLOCK),
        in_specs=[pl.BlockSpec((tm, d), lambda i, j: (i, 0)),
                  pl.BlockSpec((1, d), lambda i, j: (0, 0)),
                  pl.BlockSpec((1, d, COL_BLOCK), lambda i, j: (layer, 0, j)),
                  pl.BlockSpec((1, HEAD_DIM), lambda i, j: (0, 0))],
        out_specs=[pl.BlockSpec((tm, COL_BLOCK), lambda i, j: (i, j)),
                   pl.BlockSpec((tm, COL_BLOCK), lambda i, j: (i, 0))],
        out_shape=[jax.ShapeDtypeStruct((n, nu), F32), jax.ShapeDtypeStruct((n, COL_BLOCK), F32)],
        scratch_shapes=[pltpu.VMEM((tm, d), BF16)],
        compiler_params=_params("parallel", "arbitrary"),
        name="in_proj",
    )(x, g.reshape(1, d), w_packed, k_gain.reshape(1, HEAD_DIM))


LOG2E = 1.4426950408889634


def _neg_suffix_ones(n, extra_cols=0):
    r = lax.broadcasted_iota(jnp.int32, (n, n + extra_cols), 0)
    c = lax.broadcasted_iota(jnp.int32, (n, n + extra_cols), 1)
    return jnp.where((r > c) | (c >= n), -1.0, 0.0).astype(BF16)


def _softplus2(z2):
    return jnp.maximum(z2, 0.0) + jnp.log2(1.0 + jnp.exp2(-jnp.abs(z2)))


def _query(q, gain):
    return (_head_rms(q, gain) * (QK_SCALE * LOG2E)).astype(BF16)


def _sb_tile(qb, kt, vt, bias2, neg_after, carry, vis):
    z2 = _dot_nt(qb, kt) + bias2
    sp2 = _softplus2(z2)
    skip = sp2 if vis is None else jnp.where(vis, sp2, 0.0)
    skip_b = skip.astype(BF16)
    cu = neg_after.shape[0]
    between = []
    for ch in reversed(range(kt.shape[0] // cu)):
        cols = slice(ch * cu, (ch + 1) * cu)
        between.insert(0, _dot(skip_b[:, cols], neg_after) + carry)
        carry = carry - jnp.sum(skip[:, cols], axis=-1, keepdims=True)
    between = between[0] if len(between) == 1 else jnp.concatenate(between, axis=1)
    w = jnp.exp2(z2 - sp2 + between)
    if vis is not None:
        w = jnp.where(vis, w, 0.0)
    return _dot(w.astype(BF16), vt), carry


def _attn_prompt_kernel(q_ref, k_ref, v_ref, qg_ref, bias_ref, o_ref, qb_ref, acc_ref, car_ref, *, bq, bk, sub):
    i = pl.program_id(2)
    qb_ref[...] = _query(q_ref[...], qg_ref[...])
    bias2 = bias_ref[:, 0:1]
    neg_after = _neg_suffix_ones(sub)
    acc_ref[...] = jnp.zeros_like(acc_ref)
    car_ref[...] = jnp.zeros_like(car_ref)

    def key_block(start, row0, vis):
        kt = k_ref[pl.ds(start, bk), :].astype(BF16)
        vt = v_ref[pl.ds(start, bk), :].astype(BF16)
        rows = slice(row0, bq)
        pv, carry = _sb_tile(qb_ref[rows, :], kt, vt, bias2, neg_after, car_ref[rows, :], vis)
        acc_ref[rows, :] += pv
        car_ref[rows, :] = carry

    for d in reversed(range(bq // bk)):
        r = lax.broadcasted_iota(jnp.int32, (bq - d * bk, bk), 0)
        c = lax.broadcasted_iota(jnp.int32, (bq - d * bk, bk), 1)
        key_block(pl.multiple_of(i * bq + d * bk, bk), d * bk, c < r)

    def body(it, _):
        key_block(pl.multiple_of((i * (bq // bk) - 1 - it) * bk, bk), 0, None)
        return 0

    lax.fori_loop(0, i * (bq // bk), body, 0)
    o_ref[...] = acc_ref[...]


def _attn_prompt(u, kn, qg, bias2_vec, n_batch, seq, bq, bk, sub):
    nq = seq // bq
    hb = COL_BLOCK // HEAD_DIM
    return pl.pallas_call(
        functools.partial(_attn_prompt_kernel, bq=bq, bk=bk, sub=sub),
        grid=(n_batch, N_HEADS, nq),
        in_specs=[pl.BlockSpec((bq, HEAD_DIM), lambda b, h, i: (b * nq + i, CB_AQ * hb + h)),
                  pl.BlockSpec((seq, HEAD_DIM), lambda b, h, i: (b, h)),
                  pl.BlockSpec((seq, HEAD_DIM), lambda b, h, i: (b, CB_AV * hb + h)),
                  pl.BlockSpec((1, HEAD_DIM), lambda b, h, i: (0, 0)),
                  pl.BlockSpec((1, HEAD_DIM), lambda b, h, i: (0, h))],
        out_specs=pl.BlockSpec((bq, HEAD_DIM), lambda b, h, i: (b * nq + i, h)),
        out_shape=jax.ShapeDtypeStruct((n_batch * seq, COL_BLOCK), F32),
        scratch_shapes=[pltpu.VMEM((bq, HEAD_DIM), BF16), pltpu.VMEM((bq, HEAD_DIM), F32),
                        pltpu.VMEM((bq, 1), F32)],
        compiler_params=_params("parallel", "parallel", "arbitrary"),
        name="attn_prompt",
    )(u, kn, u, qg.reshape(1, HEAD_DIM), bias2_vec)


def _attn_sample_kernel(pt_ref, q_ref, kn_ref, vn_ref, qg_ref, bias_ref, *rest, n_new, pages_per_step):
    page_refs = rest[:2 * pages_per_step]
    o_ref, acc_ref, car_ref, knp_ref, vnp_ref = rest[2 * pages_per_step:]
    j = pl.program_id(1)
    head = lambda h: slice(h * HEAD_DIM, (h + 1) * HEAD_DIM)
    zero = jnp.zeros((n_new, HEAD_DIM), F32)
    qg = qg_ref[...]
    q_rows = [jnp.concatenate([_head_rms(q_ref[:, head(h)], qg) if g == h else zero for g in range(N_HEADS)], axis=1)
              for h in range(N_HEADS)]
    qb = (jnp.concatenate(q_rows, axis=0) * (QK_SCALE * LOG2E)).astype(BF16)
    bias2 = jnp.concatenate([jnp.broadcast_to(bias_ref[:, h * HEAD_DIM:h * HEAD_DIM + 1], (n_new, 1))
                             for h in range(N_HEADS)], axis=0)

    def attend(keys, values, neg_after, vis):
        pv, carry = _sb_tile(qb, keys, values, bias2, neg_after, car_ref[...], vis)
        car_ref[...] = carry
        for h in range(N_HEADS):
            acc_ref[:, head(h)] += pv[h * n_new:(h + 1) * n_new, head(h)]

    @pl.when(j == 0)
    def _():
        acc_ref[...] = jnp.zeros_like(acc_ref)
        car_ref[...] = jnp.zeros_like(car_ref)
        knp_ref[...] = jnp.zeros_like(knp_ref)
        vnp_ref[...] = jnp.zeros_like(vnp_ref)
        knp_ref[0:n_new, :] = kn_ref[...]
        vnp_ref[0:n_new, :] = vn_ref[...]
        r = lax.broadcasted_iota(jnp.int32, (N_HEADS * n_new, PAGE_SIZE), 0)
        c = lax.broadcasted_iota(jnp.int32, (N_HEADS * n_new, PAGE_SIZE), 1)
        attend(knp_ref[...].astype(BF16), vnp_ref[...].astype(BF16), _neg_suffix_ones(PAGE_SIZE),
               c < (r & (n_new - 1)))

    def pages(refs):
        page = lambda ref: jnp.concatenate([ref[0, pl.ds(h, PAGE_SIZE, stride=N_HEADS), :] for h in range(N_HEADS)],
                                           axis=1).astype(BF16)
        return jnp.concatenate([page(ref) for ref in reversed(refs)], axis=0)

    attend(pages(page_refs[:pages_per_step]), pages(page_refs[pages_per_step:]), _neg_suffix_ones(2 * PAGE_SIZE), None)
    o_ref[...] = acc_ref[...]


def _attn_sample(u, kn, cache_k, cache_v, page_table, layer, qg, bias2_vec, n_new, pages_per_step):
    n_dec, n_pages = page_table.shape
    depth, n_pool = cache_k.shape[0], cache_k.shape[1]
    ck = cache_k.reshape(depth * n_pool, PAGE_SIZE * N_HEADS, HEAD_DIM)
    cv = cache_v.reshape(depth * n_pool, PAGE_SIZE * N_HEADS, HEAD_DIM)
    n_steps = n_pages // pages_per_step
    base = layer * n_pool

    def page_map(p):
        def index(b, j, pt):
            logical = (n_steps - 1 - j) * pages_per_step + (pages_per_step - 1 - p)
            return (base + pt[b * n_pages + logical], 0, 0)
        return index

    page_specs = [pl.BlockSpec((1, PAGE_SIZE * N_HEADS, HEAD_DIM), page_map(p)) for p in range(pages_per_step)]
    grid_spec = pltpu.PrefetchScalarGridSpec(
        num_scalar_prefetch=1,
        grid=(n_dec, n_steps),
        in_specs=[pl.BlockSpec((n_new, COL_BLOCK), lambda b, j, pt: (b, CB_AQ)),
                  pl.BlockSpec((n_new, COL_BLOCK), lambda b, j, pt: (b, 0)),
                  pl.BlockSpec((n_new, COL_BLOCK), lambda b, j, pt: (b, CB_AV)),
                  pl.BlockSpec((1, HEAD_DIM), lambda b, j, pt: (0, 0)),
                  pl.BlockSpec((1, COL_BLOCK), lambda b, j, pt: (0, 0))] + page_specs + page_specs,
        out_specs=pl.BlockSpec((n_new, COL_BLOCK), lambda b, j, pt: (b, 0)),
        scratch_shapes=[pltpu.VMEM((n_new, COL_BLOCK), F32), pltpu.VMEM((N_HEADS * n_new, 1), F32),
                        pltpu.VMEM((PAGE_SIZE, COL_BLOCK), F32), pltpu.VMEM((PAGE_SIZE, COL_BLOCK), F32)],
    )
    return pl.pallas_call(
        functools.partial(_attn_sample_kernel, n_new=n_new, pages_per_step=pages_per_step),
        grid_spec=grid_spec,
        out_shape=jax.ShapeDtypeStruct((n_dec * n_new, COL_BLOCK), F32),
        compiler_params=_params("parallel", "arbitrary"),
        name="attn_sample",
    )(page_table.reshape(-1), u, kn, u, qg.reshape(1, HEAD_DIM), bias2_vec,
      *([ck] * pages_per_step), *([cv] * pages_per_step))


POOL_HALO = 16


def _pool_kernel(x_ref, halo_ref, w_ref, scale_ref, o_ref, buf_ref, *, tm, seq_rows, prefix_rows, pos0):
    i = pl.program_id(0)
    buf_ref[0:POOL_HALO, :] = halo_ref[...]
    buf_ref[POOL_HALO:, :] = x_ref[...]
    row = lax.broadcasted_iota(jnp.int32, (tm, LANE), 0) + i * tm
    pos = row & (seq_rows - 1)
    abs_pos = pos + (pos0 - prefix_rows)
    for g, win in enumerate(POOL_WINDOWS):
        sl = slice(g * LANE, (g + 1) * LANE)
        x = x_ref[:, sl]
        total = x
        for s in range(1, win):
            shifted = buf_ref[POOL_HALO - s:POOL_HALO - s + tm, sl]
            total = total + jnp.where(pos >= s, shifted, 0.0)
        count = jnp.minimum(abs_pos + 1, win).astype(F32)
        pooled = total / count - x
        mixed = _dot(pooled.astype(BF16), w_ref[g].astype(BF16))
        o_ref[:, sl] = mixed * scale_ref[:, sl]


def _pool(x_arr, col_block, pool_w, pool_scale, tm, seq_rows, prefix_rows, pos0):
    n = x_arr.shape[0]
    hb = tm // POOL_HALO
    return pl.pallas_call(
        functools.partial(_pool_kernel, tm=tm, seq_rows=seq_rows, prefix_rows=prefix_rows, pos0=pos0),
        grid=(n // tm,),
        in_specs=[pl.BlockSpec((tm, COL_BLOCK), lambda i: (i, col_block)),
                  pl.BlockSpec((POOL_HALO, COL_BLOCK), lambda i: (jnp.maximum(i * hb - 1, 0), col_block)),
                  pl.BlockSpec((len(POOL_WINDOWS), LANE, LANE), lambda i: (0, 0, 0)),
                  pl.BlockSpec((1, COL_BLOCK), lambda i: (0, 0))],
        out_specs=pl.BlockSpec((tm, COL_BLOCK), lambda i: (i, 0)),
        out_shape=jax.ShapeDtypeStruct((n, COL_BLOCK), F32),
        scratch_shapes=[pltpu.VMEM((tm + POOL_HALO, COL_BLOCK), F32)],
        compiler_params=_params("parallel"),
        name="pool",
    )(x_arr, x_arr, pool_w, pool_scale.reshape(1, COL_BLOCK))


def _rotary(x, cos, sin_signed):
    return x * cos + pltpu.roll(x, HEAD_DIM // 2, 1) * sin_signed


def _retention_kernel(q_ref, k_ref, v_ref, cos_ref, sin_ref, din_ref, dq_ref, dk_ref, dc_ref, ng_ref, *rest,
                      chunk, tiles_per_seq):
    carried = tiles_per_seq > 0
    if carried:
        o_ref, sout_ref, s_ref = rest
    else:
        s0_ref, o_ref, sout_ref = rest
    i = pl.program_id(0)
    n_chunks = ROW_TILE // chunk
    cos, sin = cos_ref[...], sin_ref[...]
    ng = ng_ref[...]
    if carried:
        @pl.when(i % tiles_per_seq == 0)
        def _():
            s_ref[...] = jnp.zeros_like(s_ref)
    row = lax.broadcasted_iota(jnp.int32, (ROW_TILE, LANE), 0)

    heads = range(N_HEADS)
    head = lambda h: slice(h * HEAD_DIM, (h + 1) * HEAD_DIM)
    qr = [(_rotary(q_ref[:, head(h)], cos, sin) * QK_SCALE).astype(BF16) for h in heads]
    kr = [_rotary(k_ref[:, head(h)], cos, sin) for h in heads]
    vb = [v_ref[:, head(h)].astype(BF16) for h in heads]
    s = [(_dot_nt(qr[h], kr[h].astype(BF16)) * din_ref[h]).astype(BF16) for h in heads]
    kd = [kr[h] * dk_ref[h] for h in heads]
    if carried:
        state = [s_ref[h] for h in heads]
        inter = [_dot(qr[h], state[h].astype(BF16)) for h in heads]
        new_state = [state[h] * dc_ref[h] + _dot_tn(kd[h].astype(BF16), vb[h]) for h in heads]
        for h in heads:
            s_ref[h] = new_state[h]

        @pl.when(i % tiles_per_seq == tiles_per_seq - 1)
        def _():
            for h in heads:
                sout_ref[0, h] = new_state[h]
    else:
        chunks = range(n_chunks)
        rows = lambda c: slice(c * chunk, (c + 1) * chunk)
        inter = [jnp.concatenate([_dot(qr[h][rows(c)], s0_ref[c, h].astype(BF16)) for c in chunks], axis=0)
                 for h in heads]
        for h in heads:
            for c in chunks:
                in_chunk = (row >= c * chunk) & (row < (c + 1) * chunk)
                kd_c = jnp.where(in_chunk, kd[h], 0.0).astype(BF16)
                sout_ref[c, h] = s0_ref[c, h] * dc_ref[h] + _dot_tn(kd_c, vb[h])
    for h in heads:
        o = _dot(s[h], vb[h]) + inter[h] * dq_ref[h]
        o_ref[:, head(h)] = _head_rms(o, ng)


def _retention_tables(chunk, pos_rows):
    heads = jnp.arange(N_HEADS, dtype=F32)
    log_gamma = jnp.log(1.0 - 2.0 ** (-5.0 - heads))
    r = jnp.arange(ROW_TILE)
    i = (r % chunk).astype(F32)
    diff = i[:, None] - i[None, :]
    same = (r[:, None] // chunk) == (r[None, :] // chunk)
    dec_in = jnp.where(same & (diff >= 0), jnp.exp(log_gamma[:, None, None] * jnp.maximum(diff, 0.0)), 0.0)
    dec_q = jnp.exp(log_gamma[:, None] * (i + 1.0))
    dec_k = jnp.exp(log_gamma[:, None] * (chunk - 1.0 - i))
    dec_c = jnp.exp(log_gamma * chunk)
    lanes = (N_HEADS, ROW_TILE, LANE)
    half = HEAD_DIM // 2
    inv_freq = ROPE_BASE ** (-jnp.arange(half, dtype=F32) / half)
    ang = pos_rows.astype(F32)[:, None] * inv_freq[None, :]
    cos = jnp.concatenate([jnp.cos(ang), jnp.cos(ang)], axis=-1)
    sin = jnp.concatenate([-jnp.sin(ang), jnp.sin(ang)], axis=-1)
    return (cos, sin, dec_in, jnp.broadcast_to(dec_q[:, :, None], lanes), jnp.broadcast_to(dec_k[:, :, None], lanes),
            jnp.broadcast_to(dec_c[:, None, None], (N_HEADS, 1, LANE)))


def _retention(u, norm_g, chunk, tiles_per_seq, pos_rows, s0=None):
    n = u.shape[0]
    n_tiles = n // ROW_TILE
    carried = tiles_per_seq > 0
    cos, sin, dec_in, dec_q, dec_k, dec_c = _retention_tables(chunk, pos_rows)
    rope_tiles = cos.shape[0] // ROW_TILE
    blk = lambda cb: pl.BlockSpec((ROW_TILE, COL_BLOCK), lambda i: (i, cb))
    rope = pl.BlockSpec((ROW_TILE, HEAD_DIM), lambda i: (i % rope_tiles, 0))
    table = pl.BlockSpec((N_HEADS, ROW_TILE, LANE), lambda i: (0, 0, 0))
    in_specs = [blk(CB_CQ), blk(CB_CK), blk(CB_CV), rope, rope, table, table, table,
                pl.BlockSpec((N_HEADS, 1, LANE), lambda i: (0, 0, 0)),
                pl.BlockSpec((1, HEAD_DIM), lambda i: (0, 0))]
    args = [u, u, u, cos, sin, dec_in, dec_q, dec_k, dec_c, norm_g.reshape(1, HEAD_DIM)]
    if carried:
        n_seq = n_tiles // tiles_per_seq
        state_spec = pl.BlockSpec((1, N_HEADS, HEAD_DIM, HEAD_DIM), lambda i: (i // tiles_per_seq, 0, 0, 0))
        scratch = [pltpu.VMEM((N_HEADS, HEAD_DIM, HEAD_DIM), F32)]
    else:
        n_chunks = ROW_TILE // chunk
        n_seq = n_tiles * n_chunks
        state_spec = pl.BlockSpec((n_chunks, N_HEADS, HEAD_DIM, HEAD_DIM), lambda i: (i, 0, 0, 0))
        states, layer = s0
        in_specs.append(pl.BlockSpec((None, n_chunks, N_HEADS, HEAD_DIM, HEAD_DIM), lambda i: (layer, i, 0, 0, 0)))
        args.append(states)
        scratch = []
    return pl.pallas_call(
        functools.partial(_retention_kernel, chunk=chunk, tiles_per_seq=tiles_per_seq),
        grid=(n_tiles,),
        in_specs=in_specs,
        out_specs=[pl.BlockSpec((ROW_TILE, COL_BLOCK), lambda i: (i, 0)), state_spec],
        out_shape=[jax.ShapeDtypeStruct((n, COL_BLOCK), F32),
                   jax.ShapeDtypeStruct((n_seq, N_HEADS, HEAD_DIM, HEAD_DIM), F32)],
        scratch_shapes=scratch,
        compiler_params=_params("arbitrary"),
        name="retention",
    )(*args)


CONV_HALO = 8


def _delta_kernel(xq_ref, xk_ref, xv_ref, hq_ref, hk_ref, hv_ref, ab_ref, cw_ref, alog_ref, dt_ref, ng_ref, *rest,
                  chunk, tiles_per_seq, seq_rows, prefix_rows):
    carried = tiles_per_seq > 0
    if carried:
        o_ref, sout_ref, s_ref, buf_ref = rest
    else:
        s0_ref, o_ref, sout_ref, buf_ref = rest
    i = pl.program_id(0)
    n_chunks = ROW_TILE // chunk
    ng = ng_ref[...]
    if carried:
        @pl.when(i % tiles_per_seq == 0)
        def _():
            s_ref[...] = jnp.zeros_like(s_ref)

    row = lax.broadcasted_iota(jnp.int32, (ROW_TILE, LANE), 0)
    col = lax.broadcasted_iota(jnp.int32, (ROW_TILE, LANE), 1)
    pos = (row + i * ROW_TILE) & (seq_rows - 1)
    real = pos >= prefix_rows
    shift = int(np.log2(chunk))
    same = (row >> shift) == (col >> shift)
    incl = same & (row >= col)
    strict = same & (row > col)
    incl_ones = jnp.where(incl, 1.0, 0.0).astype(BF16)
    same_ones = jnp.where(same, 1.0, 0.0).astype(BF16)

    ab = ab_ref[...]
    g = jnp.where(real, -jnp.exp(alog_ref[...]) * _softplus(ab + dt_ref[...]), 0.0)
    beta_all = jnp.where(real, jax.nn.sigmoid(ab), 0.0)
    g3 = _split3(g)
    gcum = sum(_dot(incl_ones, p) for p in g3)
    glast = sum(_dot(same_ones, p) for p in g3)
    gcum_t = gcum.T
    egc = jnp.exp(gcum)
    erest = jnp.exp(glast - gcum)
    elast = jnp.exp(glast)

    def conv(x_ref, halo_ref, lo):
        buf_ref[0:CONV_HALO, :] = halo_ref[...]
        buf_ref[CONV_HALO:, :] = x_ref[...]
        outs = []
        for h in range(N_HEADS):
            sl = slice(h * HEAD_DIM, (h + 1) * HEAD_DIM)
            wsl = slice(lo + h * HEAD_DIM, lo + (h + 1) * HEAD_DIM)
            out = x_ref[:, sl] * cw_ref[CONV_WIDTH - 1:CONV_WIDTH, wsl]
            for s in range(1, CONV_WIDTH):
                shifted = buf_ref[CONV_HALO - s:CONV_HALO - s + ROW_TILE, sl]
                out = out + jnp.where(pos >= s, shifted, 0.0) * cw_ref[CONV_WIDTH - 1 - s:CONV_WIDTH - s, wsl]
            outs.append(_silu(out))
        return outs

    cq = conv(xq_ref, hq_ref, 0)
    ck = conv(xk_ref, hk_ref, COL_BLOCK)
    cv = conv(xv_ref, hv_ref, 2 * COL_BLOCK)

    heads = range(N_HEADS)
    col = lambda m, h: m[:, h:h + 1]
    l2 = lambda x: x * lax.rsqrt(jnp.sum(x * x, axis=-1, keepdims=True) + EPS)
    qn = [l2(cq[h]) * QK_SCALE for h in heads]
    kn = [jnp.where(real, l2(ck[h]), 0.0) for h in heads]
    beta = [col(beta_all, N_HEADS + h) for h in heads]
    decay = [jnp.where(incl, jnp.exp(jnp.where(incl, col(gcum, h) - gcum_t[h:h + 1, :], 0.0)), 0.0) for h in heads]
    kbeta = [kn[h] * beta[h] for h in heads]
    knb = [kn[h].astype(BF16) for h in heads]
    a_mat = [jnp.where(strict, _dot_nt(kbeta[h].astype(BF16), knb[h]) * decay[h], 0.0) for h in heads]
    y = [-a for a in a_mat]
    p2 = [_split2(a) for a in a_mat]
    for _ in range(int(np.log2(chunk)) - 1):
        p = [_dot_split(p2[h], p2[h]) for h in heads]
        p2 = [_split2(p[h]) for h in heads]
        y = [y[h] + p[h] + _dot_split(_split2(y[h]), p2[h]) for h in heads]
    rhs = [jnp.concatenate([cv[h] * beta[h], kbeta[h] * col(egc, h)], axis=1) for h in heads]
    sol = [rhs[h] + _dot(y[h].astype(BF16), rhs[h].astype(BF16)) for h in heads]
    u_mat = [s[:, :HEAD_DIM] for s in sol]
    wb = [s[:, HEAD_DIM:].astype(BF16) for s in sol]
    attn = [(_dot_nt(qn[h].astype(BF16), knb[h]) * decay[h]).astype(BF16) for h in heads]
    qg = [(qn[h] * col(egc, h)).astype(BF16) for h in heads]
    kdec = [kn[h] * col(erest, h) for h in heads]
    if carried:
        state = [s_ref[h] for h in heads]
        sb = [st.astype(BF16) for st in state]
        vnb = [(u_mat[h] - _dot(wb[h], sb[h])).astype(BF16) for h in heads]
        o = [_dot(qg[h], sb[h]) + _dot(attn[h], vnb[h]) for h in heads]
        new_state = [state[h] * elast[0:1, h:h + 1] + _dot_tn(kdec[h].astype(BF16), vnb[h]) for h in heads]
        for h in heads:
            s_ref[h] = new_state[h]

        @pl.when(i % tiles_per_seq == tiles_per_seq - 1)
        def _():
            for h in heads:
                sout_ref[0, h] = new_state[h]
    else:
        chunks = range(n_chunks)
        rows = lambda c: slice(c * chunk, (c + 1) * chunk)
        sb = [[s0_ref[c, h].astype(BF16) for c in chunks] for h in heads]
        ws = [jnp.concatenate([_dot(wb[h][rows(c)], sb[h][c]) for c in chunks], axis=0) for h in heads]
        vnb = [(u_mat[h] - ws[h]).astype(BF16) for h in heads]
        qs = [jnp.concatenate([_dot(qg[h][rows(c)], sb[h][c]) for c in chunks], axis=0) for h in heads]
        o = [qs[h] + _dot(attn[h], vnb[h]) for h in heads]
        for h in heads:
            for c in chunks:
                in_chunk = (row >= c * chunk) & (row < (c + 1) * chunk)
                kdec_c = jnp.where(in_chunk, kdec[h], 0.0).astype(BF16)
                sout_ref[c, h] = s0_ref[c, h] * elast[c * chunk:c * chunk + 1, h:h + 1] + _dot_tn(kdec_c, vnb[h])
    for h in heads:
        o_ref[:, h * HEAD_DIM:(h + 1) * HEAD_DIM] = _head_rms(o[h], ng)


def _delta(x_arr, cb_q, ab_arr, ab_block, conv_w, a_log, dt_bias, norm_g, chunk, tiles_per_seq, seq_rows,
           prefix_rows, s0=None):
    n = x_arr.shape[0]
    n_tiles = n // ROW_TILE
    carried = tiles_per_seq > 0
    hb = ROW_TILE // CONV_HALO
    blk = lambda cb: pl.BlockSpec((ROW_TILE, COL_BLOCK), lambda i: (i, cb))
    halo = lambda cb: pl.BlockSpec((CONV_HALO, COL_BLOCK), lambda i: (jnp.maximum(i * hb - 1, 0), cb))
    vec = pl.BlockSpec((1, LANE), lambda i: (0, 0))
    pad = lambda v: jnp.zeros((1, LANE), F32).at[0, :N_HEADS].set(v)
    in_specs = [blk(cb_q), blk(cb_q + 1), blk(cb_q + 2), halo(cb_q), halo(cb_q + 1), halo(cb_q + 2),
                pl.BlockSpec((ROW_TILE, LANE), lambda i: (i, ab_block)),
                pl.BlockSpec((CONV_WIDTH, 3 * COL_BLOCK), lambda i: (0, 0)), vec, vec,
                pl.BlockSpec((1, HEAD_DIM), lambda i: (0, 0))]
    args = [x_arr] * 6 + [ab_arr, conv_w, pad(a_log), pad(dt_bias), norm_g.reshape(1, HEAD_DIM)]
    scratch = [pltpu.VMEM((ROW_TILE + CONV_HALO, COL_BLOCK), F32)]
    if carried:
        n_seq = n_tiles // tiles_per_seq
        state_spec = pl.BlockSpec((1, N_HEADS, HEAD_DIM, HEAD_DIM), lambda i: (i // tiles_per_seq, 0, 0, 0))
        scratch = [pltpu.VMEM((N_HEADS, HEAD_DIM, HEAD_DIM), F32)] + scratch
    else:
        n_chunks = ROW_TILE // chunk
        n_seq = n_tiles * n_chunks
        state_spec = pl.BlockSpec((n_chunks, N_HEADS, HEAD_DIM, HEAD_DIM), lambda i: (i, 0, 0, 0))
        states, layer = s0
        in_specs.append(pl.BlockSpec((None, n_chunks, N_HEADS, HEAD_DIM, HEAD_DIM), lambda i: (layer, i, 0, 0, 0)))
        args.append(states)
    return pl.pallas_call(
        functools.partial(_delta_kernel, chunk=chunk, tiles_per_seq=tiles_per_seq, seq_rows=seq_rows,
                          prefix_rows=prefix_rows),
        grid=(n_tiles,),
        in_specs=in_specs,
        out_specs=[pl.BlockSpec((ROW_TILE, COL_BLOCK), lambda i: (i, 0)), state_spec],
        out_shape=[jax.ShapeDtypeStruct((n, COL_BLOCK), F32),
                   jax.ShapeDtypeStruct((n_seq, N_HEADS, HEAD_DIM, HEAD_DIM), F32)],
        scratch_shapes=scratch,
        compiler_params=_params("arbitrary"),
        name="delta",
    )(*args)


def _out_proj_kernel(x_ref, oa_ref, ob_ref, oc_ref, od_ref, ga_ref, gb_ref, gc_ref, gd_ref,
                     ma_ref, mb_ref, mc_ref, md_ref, wb_ref, wo_ref, y_ref):
    mixed = None
    branches = ((oa_ref, ga_ref, ma_ref), (ob_ref, gb_ref, mb_ref), (oc_ref, gc_ref, mc_ref), (od_ref, gd_ref, md_ref))
    for n, (o_ref, g_ref, m_ref) in enumerate(branches):
        branch = (o_ref[...] * _silu(g_ref[...])).astype(BF16)
        proj = _dot(branch, wb_ref[n])
        gated = jax.nn.sigmoid(m_ref[...]) * proj
        mixed = gated if mixed is None else mixed + gated
    y_ref[...] = x_ref[...] + _dot(mixed.astype(BF16), wo_ref[...])


def _out_proj(x, u, oa, ob, oc, od, w_branch, w_out, tm):
    n = x.shape[0]
    row = lambda width, cb: pl.BlockSpec((tm, width), lambda i: (i, cb))
    merge0 = CB_MERGE * COL_BLOCK // D_MODEL
    return pl.pallas_call(
        _out_proj_kernel,
        grid=(n // tm,),
        in_specs=[row(D_MODEL, 0)] + [row(COL_BLOCK, 0)] * 4
                 + [row(COL_BLOCK, CB_AG), row(COL_BLOCK, CB_BG), row(COL_BLOCK, CB_CG), row(COL_BLOCK, CB_DG)]
                 + [row(D_MODEL, merge0 + b) for b in range(N_BRANCH)]
                 + [pl.BlockSpec((N_BRANCH, BRANCH_WIDTH, D_MODEL), lambda i: (0, 0, 0)),
                    pl.BlockSpec((D_MODEL, D_MODEL), lambda i: (0, 0))],
        out_specs=row(D_MODEL, 0),
        out_shape=jax.ShapeDtypeStruct((n, D_MODEL), F32),
        compiler_params=_params("parallel"),
        name="out_proj",
    )(x, oa, ob, oc, od, *([u] * 8), w_branch, w_out)


def _kv_leaves_kernel(*refs, tm):
    ko_ref, vo_ref = refs[-2:]
    layer = pl.program_id(0)
    for li in range(len(refs) // 2 - 1):
        @pl.when(layer == li)
        def _(k_ref=refs[2 * li], v_ref=refs[2 * li + 1]):
            for h in range(N_HEADS):
                rows = pl.ds(h, tm, stride=N_HEADS)
                ko_ref[0, rows, :] = k_ref[:, h * HEAD_DIM:(h + 1) * HEAD_DIM]
                vo_ref[0, rows, :] = v_ref[:, h * HEAD_DIM:(h + 1) * HEAD_DIM]


def _kv_leaves(kn_and_u, tm):
    depth = len(kn_and_u) // 2
    n = kn_and_u[0].shape[0]
    spec = lambda li, cb: pl.BlockSpec((tm, COL_BLOCK), lambda l, i: (jnp.where(l == li, i, 0), cb))
    in_specs = [spec(li, cb) for li in range(depth) for cb in (0, CB_AV)]
    out_spec = pl.BlockSpec((1, tm * N_HEADS, HEAD_DIM), lambda l, i: (l, i, 0))
    out_shape = jax.ShapeDtypeStruct((depth, n * N_HEADS, HEAD_DIM), F32)
    return pl.pallas_call(
        functools.partial(_kv_leaves_kernel, tm=tm),
        grid=(depth, n // tm),
        in_specs=in_specs,
        out_specs=[out_spec, out_spec],
        out_shape=[out_shape, out_shape],
        compiler_params=_params("arbitrary", "arbitrary"),
        name="kv_leaves",
    )(*kn_and_u)


def _pack_w_in(w):
    d_ab = 13 * COL_BLOCK
    tail = jnp.zeros(w.shape[:2] + (COL_BLOCK - 2 * N_HEADS,), w.dtype)
    return jnp.concatenate([w[..., :d_ab], w[..., d_ab + 2 * N_HEADS:], w[..., d_ab:d_ab + 2 * N_HEADS], tail],
                           axis=-1).astype(BF16)


def _pad_front(hist, x, seq_rows):
    n, t, w = x.shape
    p = hist.shape[1]
    zeros = jnp.zeros((n, seq_rows - t - p, w), x.dtype)
    return jnp.concatenate([zeros, hist, x], axis=1).reshape(n * seq_rows, w)


def kernel(x_prompt, x_sample, cache_k, cache_v, state_pool, state_ret, state_delta, state_conv, page_table,
           norm_g, w_in, sb_q_norm, sb_k_norm, sb_bias, pool_w, pool_scale, ret_norm,
           dn_conv_w, dn_a_log, dn_dt_bias, dn_norm, w_branch, w_out):
    n_batch, seq, _ = x_prompt.shape
    n_dec, n_new, _ = x_sample.shape
    depth = norm_g.shape[0]
    past = page_table.shape[1] * PAGE_SIZE
    heads = (N_HEADS, HEAD_DIM)
    pool_rows, delta_rows = 32, 16
    assert (n_dec * n_new) % (2 * ROW_TILE) == 0 and seq % 1024 == 0 and past >= max(POOL_WINDOWS)

    xp = x_prompt.reshape(n_batch * seq, D_MODEL)
    xs = x_sample.reshape(n_dec * n_new, D_MODEL)
    pos_p = jnp.arange(seq)
    pos_s = past + (jnp.arange(ROW_TILE) % n_new)
    new_p, new_s, kv_p = [], [], []
    w_packed = _pack_w_in(w_in)
    for l in range(depth):
        wb = w_branch[l].astype(BF16)
        wo = w_out[l].astype(BF16)
        bias2_vec = jnp.repeat(sb_bias[l] * LOG2E, HEAD_DIM).reshape(1, COL_BLOCK)
        ab_block = CB_DAB * COL_BLOCK // LANE

        u, kn = _in_proj(xp, norm_g[l], w_packed, l, sb_k_norm[l], 1024)
        oa = _attn_prompt(u, kn, sb_q_norm[l], bias2_vec, n_batch, seq, 512, 512, 256)
        ob = _pool(u, CB_BIN, pool_w[l], pool_scale[l], 512, seq, 0, 0)
        oc, ret = _retention(u, ret_norm[l], ROW_TILE, seq // ROW_TILE, pos_p)
        od, delta = _delta(u, CB_DQ, u, ab_block, dn_conv_w[l], dn_a_log[l], dn_dt_bias[l], dn_norm[l],
                           ROW_TILE, seq // ROW_TILE, seq, 0)
        xp = _out_proj(xp, u, oa, ob, oc, od, wb, wo, 256)
        last = lambda rows, cb, width: u.reshape(n_batch, seq, -1)[:, seq - rows:, cb * COL_BLOCK:cb * COL_BLOCK + width]
        kv_p += [kn, u]
        new_p.append((last(POOL_STATE, CB_BIN, COL_BLOCK), ret, delta, last(CONV_WIDTH - 1, CB_DQ, 3 * COL_BLOCK)))

        u, kn = _in_proj(xs, norm_g[l], w_packed, l, sb_k_norm[l], n_dec * n_new)
        oa = _attn_sample(u, kn, cache_k, cache_v, page_table, l, sb_q_norm[l], bias2_vec, n_new, 16)
        cols = lambda cb, width=COL_BLOCK: u[:, cb * COL_BLOCK:cb * COL_BLOCK + width].reshape(n_dec, n_new, width)
        b_in, d_qkv = cols(CB_BIN), cols(CB_DQ, 3 * COL_BLOCK)
        ob = _pool(_pad_front(state_pool[l], b_in, pool_rows), 0, pool_w[l], pool_scale[l], 256, pool_rows,
                   pool_rows - n_new, past)
        ob = ob.reshape(n_dec, pool_rows, COL_BLOCK)[:, -n_new:].reshape(n_dec * n_new, COL_BLOCK)
        oc, ret = _retention(u, ret_norm[l], n_new, 0, pos_s, (state_ret, l))
        d_ab = u[:, CB_DAB * COL_BLOCK:CB_DAB * COL_BLOCK + LANE].reshape(n_dec, n_new, LANE)
        od, delta = _delta(_pad_front(state_conv[l], d_qkv, delta_rows), 0,
                           _pad_front(jnp.zeros((n_dec, 0, LANE), F32), d_ab, delta_rows), 0,
                           dn_conv_w[l], dn_a_log[l], dn_dt_bias[l], dn_norm[l],
                           delta_rows, 0, delta_rows, delta_rows - n_new, (state_delta, l))
        od = od.reshape(n_dec, delta_rows, COL_BLOCK)[:, -n_new:].reshape(n_dec * n_new, COL_BLOCK)
        xs = _out_proj(xs, u, oa, ob, oc, od, wb, wo, 256)
        new_s.append((kn.reshape(n_dec, n_new, *heads), cols(CB_AV).reshape(n_dec, n_new, *heads),
                      jnp.concatenate([state_pool[l], b_in], axis=1)[:, -POOL_STATE:], ret, delta,
                      jnp.concatenate([state_conv[l], d_qkv], axis=1)[:, -(CONV_WIDTH - 1):]))

    stack = lambda states, i: jnp.stack([s[i] for s in states], axis=0)
    k_prompt, v_prompt = _kv_leaves(kv_p, 512)
    return (xp.reshape(n_batch, seq, D_MODEL), xs.reshape(n_dec, n_new, D_MODEL),
            k_prompt.reshape(depth, n_batch, seq, *heads), v_prompt.reshape(depth, n_batch, seq, *heads),
            *[stack(new_p, i) for i in range(4)], *[stack(new_s, i) for i in range(6)])
```
